```python
import jax, jax.numpy as jnp
from jax import lax
import numpy as np

D_MODEL = 1024
BATCH = 8
SEQ = 2048
DEPTH = 4
DEC_BATCH = 128
DEC_SEQ = 1
PAST_LEN = 2048
PAGE_SIZE = 128

N_MIXERS = 4
N_LAYERS_A = (DEPTH + 3) // 4
N_LAYERS_B = (DEPTH + 2) // 4
N_LAYERS_C = (DEPTH + 1) // 4
N_LAYERS_D = DEPTH // 4

PLE_DIM = 256
WIDTH_A = 2 * D_MODEL
GROUPS_A = 8
CHUNK_A = 128
WIDTH_B = 2 * D_MODEL
CONV_W = 31
N_HEADS = 8
HEAD_DIM = D_MODEL // N_HEADS
WIDTH_ATT = N_HEADS * HEAD_DIM
ROPE_DIM = HEAD_DIM // 4
ROPE_THETA = 500000.0
MOBA_BLOCK = 256
MOBA_TOPK = 3
MOBA_Q_CHUNK = 8
FOX_Q_BLOCK = 128
EPS = 1e-6

kernel_name = "hybrid_gmlp_conformer_moba_fox_step"


def rms_norm(x, g):
    xf = x.astype(jnp.float32)
    y = xf * lax.rsqrt(jnp.mean(xf * xf, axis=-1, keepdims=True) + EPS)
    return (y * g.astype(jnp.float32)).astype(x.dtype)


def layer_norm(x, g, b):
    xf = x.astype(jnp.float32)
    xc = xf - jnp.mean(xf, axis=-1, keepdims=True)
    y = xc * lax.rsqrt(jnp.mean(xc * xc, axis=-1, keepdims=True) + EPS)
    return (y * g.astype(jnp.float32) + b.astype(jnp.float32)).astype(x.dtype)


def partial_rope(x, pos):
    half = ROPE_DIM // 2
    inv = ROPE_THETA ** (-jnp.arange(half, dtype=jnp.float32) * 2.0 / ROPE_DIM)
    ang = pos.astype(jnp.float32)[:, None] * inv[None, :]
    cos = jnp.cos(ang)[None, :, None, :]
    sin = jnp.sin(ang)[None, :, None, :]
    xr = x[..., :ROPE_DIM].astype(jnp.float32)
    x1, x2 = xr[..., :half], xr[..., half:]
    rot = jnp.concatenate([x1 * cos - x2 * sin, x2 * cos + x1 * sin], axis=-1)
    return jnp.concatenate([rot.astype(x.dtype), x[..., ROPE_DIM:]], axis=-1)


def gather_pages(pool, page_table):
    g = pool[page_table]
    return g.reshape((g.shape[0], g.shape[1] * g.shape[2]) + g.shape[3:])


def chunk_mlp_mixer(xn, w_in, ln_g, ln_b, w_s, b_s, w_out):
    B, T, _ = xn.shape
    u, v, g = jnp.split(xn @ w_in, 3, axis=-1)
    u = jax.nn.gelu(u)
    v = layer_norm(jax.nn.gelu(v), ln_g, ln_b)
    c = min(CHUNK_A, T)
    w = jnp.tril(w_s[:, :c, :c])
    vc = v.reshape(B, T // c, c, GROUPS_A, WIDTH_A // GROUPS_A)
    s = jnp.einsum('gts,bnsgd->bntgd', w, vc) + jnp.transpose(b_s[:, :c])[None, None, :, :, None]
    y = u * s.reshape(B, T, WIDTH_A) * jax.nn.silu(g)
    return y @ w_out, v[:, T - c:]


def conformer_conv_mixer(xn, prefix, w_in, conv_w, conv_b, ln_g, ln_b, w_out):
    a, b, g = jnp.split(xn @ w_in, 3, axis=-1)
    h = a * jax.nn.sigmoid(b)
    h_ext = jnp.concatenate([prefix.astype(h.dtype), h], axis=1)
    c = lax.conv_general_dilated(h_ext, conv_w[:, None, :].astype(h.dtype), (1,), 'VALID',
                                 dimension_numbers=('NWC', 'WIO', 'NWC'),
                                 feature_group_count=WIDTH_B) + conv_b
    c = jax.nn.silu(layer_norm(c, ln_g, ln_b))
    return (c * jax.nn.silu(g)) @ w_out, h_ext[:, -(CONV_W - 1):]


def moba_attention(q, k, v, pos0):
    B, T, H, Dh = q.shape
    L = k.shape[1]
    nb = -(-L // MOBA_BLOCK)
    kpad = nb * MOBA_BLOCK - L
    k = jnp.pad(k, ((0, 0), (0, kpad), (0, 0), (0, 0))).reshape(B, nb, MOBA_BLOCK, H, Dh)
    v = jnp.pad(v, ((0, 0), (0, kpad), (0, 0), (0, 0))).reshape(B, nb, MOBA_BLOCK, H, Dh)
    kmean = jnp.mean(k.astype(jnp.float32), axis=2)
    n_sel = min(MOBA_TOPK, nb)
    qc = min(MOBA_Q_CHUNK, T)
    nc = -(-T // qc)
    qs = jnp.pad(q, ((0, 0), (0, nc * qc - T), (0, 0), (0, 0)))
    qs = qs.reshape(B, nc, qc, H, Dh).transpose(1, 0, 3, 2, 4)
    ps = jnp.minimum(pos0 + jnp.arange(nc * qc), L - 1).reshape(nc, qc)
    b_ix = jnp.arange(B)[:, None, None, None]
    h_ix = jnp.arange(H)[None, :, None, None]
    blk_ix = jnp.arange(nb)
    off = jnp.arange(MOBA_BLOCK)
    scale = HEAD_DIM ** -0.5

    def one_chunk(args):
        qi, pi = args
        own = pi // MOBA_BLOCK
        gate = jnp.einsum('bhqd,bnhd->bhqn', qi.astype(jnp.float32), kmean)
        gate = jnp.where(blk_ix[None, None, None, :] < own[None, None, :, None], gate, -jnp.inf)
        _, top = lax.top_k(gate, n_sel)
        own_b = jnp.broadcast_to(own[None, None, :, None], (B, H, qc, 1)).astype(top.dtype)
        idx = jnp.concatenate([top, own_b], axis=-1)
        ok = jnp.concatenate([top < own_b, jnp.ones(own_b.shape, dtype=bool)], axis=-1)
        kg = k[b_ix, idx, :, h_ix]
        vg = v[b_ix, idx, :, h_ix]
        s = jnp.einsum('bhqd,bhqnkd->bhqnk', qi, kg).astype(jnp.float32) * scale
        kpos = idx[..., None] * MOBA_BLOCK + off
        s = jnp.where(ok[..., None] & (kpos <= pi[None, None, :, None, None]), s, -jnp.inf)
        pr = jax.nn.softmax(s.reshape(B, H, qc, -1), axis=-1).reshape(s.shape).astype(vg.dtype)
        return jnp.einsum('bhqnk,bhqnkd->bhqd', pr, vg)

    o = lax.map(one_chunk, (qs, ps))
    return o.transpose(1, 0, 3, 2, 4).reshape(B, nc * qc, H, Dh)[:, :T]


def moba_mixer(xn, k_past, v_past, pos0, w_in, qn_g, kn_g, w_out):
    B, T, _ = xn.shape
    q, k, v, g = jnp.split(xn @ w_in, 4, axis=-1)
    pos = pos0 + jnp.arange(T)
    q = partial_rope(rms_norm(q.reshape(B, T, N_HEADS, HEAD_DIM), qn_g), pos)
    k = partial_rope(rms_norm(k.reshape(B, T, N_HEADS, HEAD_DIM), kn_g), pos)
    v = v.reshape(B, T, N_HEADS, HEAD_DIM)
    k_all = jnp.concatenate([k_past.astype(k.dtype), k], axis=1)
    v_all = jnp.concatenate([v_past.astype(v.dtype), v], axis=1)
    o = moba_attention(q, k_all, v_all, pos0)
    y = o.reshape(B, T, WIDTH_ATT) * jax.nn.silu(g)
    return y @ w_out, k, v


def forgetting_attention(q, k, v, cum, pos0):
    B, T, H, Dh = q.shape
    L = k.shape[1]
    qb = min(FOX_Q_BLOCK, T)
    nq = -(-T // qb)
    pad = nq * qb - T
    qs = jnp.pad(q, ((0, 0), (0, pad), (0, 0), (0, 0))).reshape(B, nq, qb, H, Dh).transpose(1, 0, 3, 2, 4)
    cqs = jnp.pad(cum[:, pos0:pos0 + T], ((0, 0), (0, pad), (0, 0))).reshape(B, nq, qb, H).transpose(1, 0, 3, 2)
    ps = (pos0 + jnp.arange(nq * qb)).reshape(nq, qb)
    kh = k.transpose(0, 2, 1, 3)
    vh = v.transpose(0, 2, 1, 3)
    ck = cum.transpose(0, 2, 1)
    kpos = jnp.arange(L)
    scale = HEAD_DIM ** -0.5

    def one_block(args):
        qi, ci, pi = args
        s = jnp.einsum('bhqd,bhkd->bhqk', qi, kh).astype(jnp.float32) * scale
        s = s + ci[..., None] - ck[:, :, None, :]
        s = jnp.where(kpos[None, None, None, :] <= pi[None, None, :, None], s, -jnp.inf)
        pr = jax.nn.softmax(s, axis=-1).astype(vh.dtype)
        return jnp.einsum('bhqk,bhkd->bhqd', pr, vh)

    o = lax.map(one_block, (qs, cqs, ps))
    return o.transpose(1, 0, 3, 2, 4).reshape(B, nq * qb, H, Dh)[:, :T]


def forgetting_mixer(xn, k_past, v_past, lf_past, pos0, w_in, b_f, qn_g, kn_g, w_out):
    B, T, _ = xn.shape
    W = WIDTH_ATT
    q, k, v, g, f_logit = jnp.split(xn @ w_in, [W, 2 * W, 3 * W, 4 * W], axis=-1)
    logf = jax.nn.log_sigmoid(f_logit.astype(jnp.float32) + b_f.astype(jnp.float32))
    q = rms_norm(q.reshape(B, T, N_HEADS, HEAD_DIM), qn_g)
    k = rms_norm(k.reshape(B, T, N_HEADS, HEAD_DIM), kn_g)
    v = v.reshape(B, T, N_HEADS, HEAD_DIM)
    k_all = jnp.concatenate([k_past.astype(k.dtype), k], axis=1)
    v_all = jnp.concatenate([v_past.astype(v.dtype), v], axis=1)
    lf_all = jnp.concatenate([lf_past.astype(jnp.float32), logf], axis=1)
    cum = jnp.cumsum(lf_all, axis=1)
    o = forgetting_attention(q, k_all, v_all, cum, pos0)
    y = o.reshape(B, T, WIDTH_ATT) * jax.nn.silu(g)
    return y @ w_out, k, v, logf.astype(xn.dtype)


def setup_inputs(seed: int = 0) -> dict:
    key = jax.random.key(seed)
    keys = list(jax.random.split(key, 64))

    def nrm(shape, scale=1.0):
        return jax.random.normal(keys.pop(), shape, jnp.float32) * scale

    def gain(shape):
        return 1.0 + nrm(shape, 0.02)

    n_pages = PAST_LEN // PAGE_SIZE
    n_used = DEC_BATCH * n_pages
    n_pool = n_used + max(1, n_used // 4)
    page_table = jax.random.permutation(keys.pop(), n_pool)[:n_used].reshape(DEC_BATCH, n_pages).astype(jnp.int32)
    kv_c = (N_LAYERS_C, n_pool, PAGE_SIZE, N_HEADS, HEAD_DIM)
    kv_d = (N_LAYERS_D, n_pool, PAGE_SIZE, N_HEADS, HEAD_DIM)
    d_in = D_MODEL ** -0.5
    return {
        "x_prompt": nrm((BATCH, SEQ, D_MODEL)),
        "x_sample": nrm((DEC_BATCH, DEC_SEQ, D_MODEL)),
        "state_conv_b": nrm((N_LAYERS_B, DEC_BATCH, CONV_W - 1, WIDTH_B), 0.5),
        "cache_k_c": nrm(kv_c),
        "cache_v_c": nrm(kv_c),
        "cache_k_d": nrm(kv_d),
        "cache_v_d": nrm(kv_d),
        "cache_logf_d": jax.nn.log_sigmoid(2.0 + nrm((N_LAYERS_D, n_pool, PAGE_SIZE, N_HEADS), 0.5)),
        "page_table": page_table,
        "p_prompt": nrm((DEPTH, BATCH, SEQ, PLE_DIM)),
        "p_sample": nrm((DEPTH, DEC_BATCH, DEC_SEQ, PLE_DIM)),
        "norm_a": gain((N_LAYERS_A, D_MODEL)),
        "w_in_a": nrm((N_LAYERS_A, D_MODEL, 3 * WIDTH_A), d_in),
        "ln_g_a": gain((N_LAYERS_A, WIDTH_A)),
        "ln_b_a": nrm((N_LAYERS_A, WIDTH_A), 0.02),
        "w_s_a": nrm((N_LAYERS_A, GROUPS_A, CHUNK_A, CHUNK_A), CHUNK_A ** -0.5),
        "b_s_a": 1.0 + nrm((N_LAYERS_A, GROUPS_A, CHUNK_A), 0.1),
        "w_out_a": nrm((N_LAYERS_A, WIDTH_A, D_MODEL), WIDTH_A ** -0.5),
        "norm_b": gain((N_LAYERS_B, D_MODEL)),
        "w_in_b": nrm((N_LAYERS_B, D_MODEL, 3 * WIDTH_B), d_in),
        "conv_w_b": nrm((N_LAYERS_B, CONV_W, WIDTH_B), CONV_W ** -0.5),
        "conv_b_b": nrm((N_LAYERS_B, WIDTH_B), 0.02),
        "ln_g_b": gain((N_LAYERS_B, WIDTH_B)),
        "ln_b_b": nrm((N_LAYERS_B, WIDTH_B), 0.02),
        "w_out_b": nrm((N_LAYERS_B, WIDTH_B, D_MODEL), WIDTH_B ** -0.5),
        "norm_c": gain((N_LAYERS_C, D_MODEL)),
        "w_in_c": nrm((N_LAYERS_C, D_MODEL, 4 * WIDTH_ATT), d_in),
        "qn_c": gain((N_LAYERS_C, HEAD_DIM)),
        "kn_c": gain((N_LAYERS_C, HEAD_DIM)),
        "w_out_c": nrm((N_LAYERS_C, WIDTH_ATT, D_MODEL), WIDTH_ATT ** -0.5),
        "norm_d": gain((N_LAYERS_D, D_MODEL)),
        "w_in_d": nrm((N_LAYERS_D, D_MODEL, 4 * WIDTH_ATT + N_HEADS), d_in),
        "b_f_d": 2.0 + nrm((N_LAYERS_D, N_HEADS), 0.5),
        "qn_d": gain((N_LAYERS_D, HEAD_DIM)),
        "kn_d": gain((N_LAYERS_D, HEAD_DIM)),
        "w_out_d": nrm((N_LAYERS_D, WIDTH_ATT, D_MODEL), WIDTH_ATT ** -0.5),
        "pe_norm": gain((DEPTH, D_MODEL)),
        "w_pe_gate": nrm((DEPTH, D_MODEL, D_MODEL), d_in),
        "w_pe_proj": nrm((DEPTH, PLE_DIM, D_MODEL), PLE_DIM ** -0.5),
    }


def reference(x_prompt, x_sample, state_conv_b, cache_k_c, cache_v_c, cache_k_d, cache_v_d, cache_logf_d,
              page_table, p_prompt, p_sample,
              norm_a, w_in_a, ln_g_a, ln_b_a, w_s_a, b_s_a, w_out_a,
              norm_b, w_in_b, conv_w_b, conv_b_b, ln_g_b, ln_b_b, w_out_b,
              norm_c, w_in_c, qn_c, kn_c, w_out_c,
              norm_d, w_in_d, b_f_d, qn_d, kn_d, w_out_d,
              pe_norm, w_pe_gate, w_pe_proj):

    def trunk(x, p, pos0, conv_prefix, past_c, past_d):
        st_a, st_b, st_kc, st_vc, st_kd, st_vd, st_lf = [], [], [], [], [], [], []
        for i in range(DEPTH):
            j = i // N_MIXERS
            kind = i % N_MIXERS
            if kind == 0:
                y, va = chunk_mlp_mixer(rms_norm(x, norm_a[j]), w_in_a[j], ln_g_a[j], ln_b_a[j],
                                        w_s_a[j], b_s_a[j], w_out_a[j])
                st_a.append(va)
            elif kind == 1:
                y, cb = conformer_conv_mixer(rms_norm(x, norm_b[j]), conv_prefix[j], w_in_b[j], conv_w_b[j],
                                             conv_b_b[j], ln_g_b[j], ln_b_b[j], w_out_b[j])
                st_b.append(cb)
            elif kind == 2:
                y, kc, vc = moba_mixer(rms_norm(x, norm_c[j]), past_c[j][0], past_c[j][1], pos0,
                                       w_in_c[j], qn_c[j], kn_c[j], w_out_c[j])
                st_kc.append(kc)
                st_vc.append(vc)
            else:
                y, kd, vd, lf = forgetting_mixer(rms_norm(x, norm_d[j]), past_d[j][0], past_d[j][1], past_d[j][2],
                                                 pos0, w_in_d[j], b_f_d[j], qn_d[j], kn_d[j], w_out_d[j])
                st_kd.append(kd)
                st_vd.append(vd)
                st_lf.append(lf)
            x = x + y
            gate = jax.nn.sigmoid(rms_norm(x, pe_norm[i]) @ w_pe_gate[i])
            x = x + gate * (p[i] @ w_pe_proj[i])
        return (x, jnp.stack(st_a), jnp.stack(st_b), jnp.stack(st_kc), jnp.stack(st_vc),
                jnp.stack(st_kd), jnp.stack(st_vd), jnp.stack(st_lf))

    bp = x_prompt.shape[0]
    dt = x_prompt.dtype
    empty_kv = jnp.zeros((bp, 0, N_HEADS, HEAD_DIM), dt)
    empty_lf = jnp.zeros((bp, 0, N_HEADS), dt)
    (y_prompt, v_a_prompt, conv_b_prompt, k_c_prompt, v_c_prompt,
     k_d_prompt, v_d_prompt, logf_d_prompt) = trunk(
        x_prompt, p_prompt, 0,
        [jnp.zeros((bp, CONV_W - 1, WIDTH_B), dt) for _ in range(N_LAYERS_B)],
        [(empty_kv, empty_kv) for _ in range(N_LAYERS_C)],
        [(empty_kv, empty_kv, empty_lf) for _ in range(N_LAYERS_D)])

    past_len = page_table.shape[1] * PAGE_SIZE
    (y_sample, v_a_sample, conv_b_sample, k_c_sample, v_c_sample,
     k_d_sample, v_d_sample, logf_d_sample) = trunk(
        x_sample, p_sample, past_len,
        [state_conv_b[j] for j in range(N_LAYERS_B)],
        [(gather_pages(cache_k_c[j], page_table), gather_pages(cache_v_c[j], page_table))
         for j in range(N_LAYERS_C)],
        [(gather_pages(cache_k_d[j], page_table), gather_pages(cache_v_d[j], page_table),
          gather_pages(cache_logf_d[j], page_table)) for j in range(N_LAYERS_D)])

    return (y_prompt, y_sample, v_a_prompt, v_a_sample, conv_b_prompt, conv_b_sample,
            k_c_prompt, v_c_prompt, k_c_sample, v_c_sample,
            k_d_prompt, v_d_prompt, logf_d_prompt, k_d_sample, v_d_sample, logf_d_sample)
```

```python
import functools
import math

import jax
import jax.numpy as jnp
import numpy as np
from jax import lax
from jax.experimental import pallas as pl
from jax.experimental.pallas import tpu as pltpu

F32 = jnp.float32
BF16 = jnp.bfloat16

D_MODEL = 1024
PLE_DIM = 256
WIDTH_A = 2048
GROUPS_A = 8
CHUNK_A = 128
WIDTH_B = 2048
CONV_W = 31
N_HEADS = 8
HEAD_DIM = 128
WIDTH_ATT = N_HEADS * HEAD_DIM
ROPE_DIM = HEAD_DIM // 4
ROPE_THETA = 500000.0
MOBA_BLOCK = 256
MOBA_TOPK = 3
PAGE_SIZE = 128
EPS = 1e-6

LANES = 128
TM = 256
CONV_PAD = 32
NEG = -1e30
SCALE = HEAD_DIM ** -0.5
VMEM_LIMIT = 60 * 1024 * 1024


def _cparams(*sem):
    return pltpu.CompilerParams(dimension_semantics=sem, vmem_limit_bytes=VMEM_LIMIT)


def _const_spec(shape):
    nd = len(shape)
    return pl.BlockSpec(shape, lambda *_: (0,) * nd, pipeline_mode=pl.Buffered(1))


def _dot(a, b):
    return jnp.dot(a, b, preferred_element_type=F32)


def _dot_nt(a, b):
    return lax.dot_general(a, b, (((1,), (1,)), ((), ())), preferred_element_type=F32)


def _sigmoid(x):
    return 1.0 / (1.0 + jnp.exp(-x))


def _silu(x):
    return x * _sigmoid(x)


def _gelu(x):
    return 0.5 * x * (1.0 + jnp.tanh(math.sqrt(2.0 / math.pi) * (x + 0.044715 * (x * x * x))))


def _rms(x, g):
    return x * lax.rsqrt(jnp.mean(x * x, axis=-1, keepdims=True) + EPS) * g


def _layer_norm(x, g, b):
    xc = x - jnp.mean(x, axis=-1, keepdims=True)
    return xc * lax.rsqrt(jnp.mean(xc * xc, axis=-1, keepdims=True) + EPS) * g + b


def _log_sigmoid(x):
    return jnp.minimum(x, 0.0) - jnp.log(1.0 + jnp.exp(-jnp.abs(x)))


def _split3(x):
    hi = x.astype(BF16)
    r = x - hi.astype(F32)
    mid = r.astype(BF16)
    lo = (r - mid.astype(F32)).astype(BF16)
    return hi, mid, lo


def _pe_update(x1, p_ref, peg_ref, wpg_ref, wpp_ref):
    xn = _rms(x1, peg_ref[...]).astype(BF16)
    gate = _sigmoid(_dot(xn, wpg_ref[...]))
    return x1 + gate * _dot(p_ref[...].astype(BF16), wpp_ref[...])


def _layer_a_kernel(x_ref, p_ref, ng_ref, win_ref, lng_ref, lnb_ref, ws_ref, bs_ref, wout_ref,
                    peg_ref, wpg_ref, wpp_ref, xo_ref, vo_ref, *scratch, chunked):
    x = x_ref[...]
    rows = x.shape[0]
    xn = _rms(x, ng_ref[...]).astype(BF16)
    w = WIDTH_A
    u = _gelu(_dot(xn, win_ref[:, 0:w]))
    v = _layer_norm(_gelu(_dot(xn, win_ref[:, w:2 * w])), lng_ref[...], lnb_ref[...])
    sg = _silu(_dot(xn, win_ref[:, 2 * w:3 * w]))
    if chunked:
        (s_ref,) = scratch
        vo_ref[0] = v[rows - CHUNK_A:, :]
        vb = v.astype(BF16)
        gw = WIDTH_A // GROUPS_A
        for c in range(rows // CHUNK_A):
            rs = slice(c * CHUNK_A, (c + 1) * CHUNK_A)
            for g in range(GROUPS_A):
                cs = slice(g * gw, (g + 1) * gw)
                s_ref[rs, cs] = _dot(ws_ref[g], vb[rs, cs]) + bs_ref[:, cs]
        s = s_ref[...]
    else:
        vo_ref[...] = v
        s = v * ws_ref[...] + bs_ref[...]
    y = (u * s * sg).astype(BF16)
    x1 = x + _dot(y, wout_ref[...])
    xo_ref[...] = _pe_update(x1, p_ref, peg_ref, wpg_ref, wpp_ref)


def _layer_a(x, p, wts, *, batch, seq):
    n = x.shape[0]
    chunked = seq > 1
    ng, win, lng, lnb, ws, bs, wout, peg, wpg, wpp = wts
    if chunked:
        tiles = seq // TM
        grid = (batch, tiles)
        row = lambda b, t: (b * tiles + t, 0)
        rows = TM
        vo_shape = jax.ShapeDtypeStruct((batch, CHUNK_A, WIDTH_A), F32)
        vo_spec = pl.BlockSpec((1, CHUNK_A, WIDTH_A), lambda b, t: (b, 0, 0))
        scratch = [pltpu.VMEM((TM, WIDTH_A), F32)]
    else:
        grid = (1, 1)
        row = lambda b, t: (0, 0)
        rows = n
        vo_shape = jax.ShapeDtypeStruct((n, WIDTH_A), F32)
        vo_spec = pl.BlockSpec((n, WIDTH_A), row)
        scratch = []
    consts = [ng, win, lng, lnb, ws, bs, wout, peg, wpg, wpp]
    return pl.pallas_call(
        functools.partial(_layer_a_kernel, chunked=chunked),
        out_shape=(jax.ShapeDtypeStruct((n, D_MODEL), F32), vo_shape),
        grid=grid,
        in_specs=[pl.BlockSpec((rows, D_MODEL), row), pl.BlockSpec((rows, PLE_DIM), row)]
        + [_const_spec(c.shape) for c in consts],
        out_specs=(pl.BlockSpec((rows, D_MODEL), row), vo_spec),
        scratch_shapes=scratch,
        compiler_params=_cparams("arbitrary", "arbitrary"),
        name="layer_a_prompt" if chunked else "layer_a_sample",
    )(x, p, *consts)


CONV_ROWS = 64


def _layer_b_prompt_kernel(x_ref, p_ref, ng_ref, win_ref, cw_ref, cb_ref, lng_ref, lnb_ref, wout_ref,
                           peg_ref, wpg_ref, wpp_ref, xo_ref, st_ref, hext_ref, c_ref):
    t = pl.program_id(1)
    ncb = WIDTH_B // LANES

    @pl.when(t == 0)
    def _():
        hext_ref[:, 0:CONV_PAD, :] = jnp.zeros((ncb, CONV_PAD, LANES), F32)

    x = x_ref[...]
    xn = _rms(x, ng_ref[...]).astype(BF16)
    w = WIDTH_B
    h = _dot(xn, win_ref[:, 0:w]) * _sigmoid(_dot(xn, win_ref[:, w:2 * w]))
    sg = _silu(_dot(xn, win_ref[:, 2 * w:3 * w]))
    for cb in range(ncb):
        hext_ref[cb, CONV_PAD:CONV_PAD + TM, :] = h[:, cb * LANES:(cb + 1) * LANES]
    st_ref[0] = h[TM - (CONV_W - 1):, :]

    base = CONV_PAD - (CONV_W - 1)

    def col_body(cb, carry):
        for rg in range(TM // CONV_ROWS):
            accs = [None] * (CONV_ROWS // 8)
            for k in range(CONV_W):
                wk = cw_ref[cb, k:k + 1, :]
                for j in range(CONV_ROWS // 8):
                    r0 = rg * CONV_ROWS + j * 8 + base + k
                    term = wk * hext_ref[cb, r0:r0 + 8, :]
                    accs[j] = term if accs[j] is None else accs[j] + term
            for j in range(CONV_ROWS // 8):
                r0 = rg * CONV_ROWS + j * 8
                c_ref[cb, r0:r0 + 8, :] = accs[j]
        return carry

    lax.fori_loop(0, ncb, col_body, 0)

    for cb in range(ncb):
        hext_ref[cb, 0:CONV_PAD, :] = hext_ref[cb, TM:TM + CONV_PAD, :]
    c = jnp.concatenate([c_ref[cb] for cb in range(ncb)], axis=-1) + cb_ref[...]
    c = _silu(_layer_norm(c, lng_ref[...], lnb_ref[...]))
    y = (c * sg).astype(BF16)
    x1 = x + _dot(y, wout_ref[...])
    xo_ref[...] = _pe_update(x1, p_ref, peg_ref, wpg_ref, wpp_ref)


def _layer_b_prompt(x, p, wts, *, batch, seq):
    n = x.shape[0]
    tiles = seq // TM
    ng, win, cw, cb, lng, lnb, wout, peg, wpg, wpp = wts
    ncb = WIDTH_B // LANES
    cw3 = jnp.transpose(cw.reshape(CONV_W, ncb, LANES), (1, 0, 2))
    consts = [ng, win, cw3, cb, lng, lnb, wout, peg, wpg, wpp]
    row = lambda b, t: (b * tiles + t, 0)
    return pl.pallas_call(
        _layer_b_prompt_kernel,
        out_shape=(jax.ShapeDtypeStruct((n, D_MODEL), F32),
                   jax.ShapeDtypeStruct((batch, CONV_W - 1, WIDTH_B), F32)),
        grid=(batch, tiles),
        in_specs=[pl.BlockSpec((TM, D_MODEL), row), pl.BlockSpec((TM, PLE_DIM), row)]
        + [_const_spec(c.shape) for c in consts],
        out_specs=(pl.BlockSpec((TM, D_MODEL), row),
                   pl.BlockSpec((1, CONV_W - 1, WIDTH_B), lambda b, t: (b, 0, 0))),
        scratch_shapes=[pltpu.VMEM((ncb, TM + CONV_PAD, LANES), F32), pltpu.VMEM((ncb, TM, LANES), F32)],
        compiler_params=_cparams("arbitrary", "arbitrary"),
        name="layer_b_prompt",
    )(x, p, *consts)


SAMPLE_CONV_ROWS = 16


def _layer_b_sample_kernel(x_ref, p_ref, st_ref, ng_ref, win_ref, cw_ref, cb_ref, lng_ref, lnb_ref, wout_ref,
                           peg_ref, wpg_ref, wpp_ref, xo_ref, ho_ref, h_ref, sg_ref, c_ref):
    i = pl.program_id(0)
    w = WIDTH_B

    @pl.when(i == 0)
    def _():
        xn = _rms(x_ref[...], ng_ref[...]).astype(BF16)
        h = _dot(xn, win_ref[:, 0:w]) * _sigmoid(_dot(xn, win_ref[:, w:2 * w]))
        h_ref[...] = h
        ho_ref[...] = h
        sg_ref[...] = _silu(_dot(xn, win_ref[:, 2 * w:3 * w]))

    r0 = pl.multiple_of(i * SAMPLE_CONV_ROWS, SAMPLE_CONV_ROWS)
    acc = cb_ref[...] + cw_ref[CONV_W - 1:CONV_W, :] * h_ref[pl.ds(r0, SAMPLE_CONV_ROWS), :]
    for k in range(CONV_W - 1):
        acc = acc + cw_ref[k:k + 1, :] * st_ref[:, k, :]
    c_ref[pl.ds(r0, SAMPLE_CONV_ROWS), :] = acc

    @pl.when(i == pl.num_programs(0) - 1)
    def _():
        c = _silu(_layer_norm(c_ref[...], lng_ref[...], lnb_ref[...]))
        y = (c * sg_ref[...]).astype(BF16)
        x1 = x_ref[...] + _dot(y, wout_ref[...])
        xo_ref[...] = _pe_update(x1, p_ref, peg_ref, wpg_ref, wpp_ref)


def _layer_b_sample(x, p, state, wts):
    n = x.shape[0]
    consts = list(wts)
    full = lambda i: (0, 0)
    return pl.pallas_call(
        _layer_b_sample_kernel,
        out_shape=(jax.ShapeDtypeStruct((n, D_MODEL), F32), jax.ShapeDtypeStruct((n, WIDTH_B), F32)),
        grid=(n // SAMPLE_CONV_ROWS,),
        in_specs=[pl.BlockSpec((n, D_MODEL), full), pl.BlockSpec((n, PLE_DIM), full),
                  pl.BlockSpec((SAMPLE_CONV_ROWS, CONV_W - 1, WIDTH_B), lambda i: (i, 0, 0))]
        + [_const_spec(c.shape) for c in consts],
        out_specs=(pl.BlockSpec((n, D_MODEL), full), pl.BlockSpec((n, WIDTH_B), full)),
        scratch_shapes=[pltpu.VMEM((n, WIDTH_B), F32)] * 3,
        compiler_params=_cparams("arbitrary"),
        name="layer_b_sample",
    )(x, p, state, *consts)


def _head_rms(xh, g):
    return xh * lax.rsqrt(jnp.mean(xh * xh, axis=-1, keepdims=True) + EPS) * g


def _rope(xh, cos, sin):
    half = ROPE_DIM // 2
    lane = lax.broadcasted_iota(jnp.int32, xh.shape, 1)
    partner = jnp.where(lane < half, pltpu.roll(xh, LANES - half, 1), pltpu.roll(xh, half, 1))
    return xh * cos + partner * sin


def _proj_c_kernel(x_ref, ng_ref, win_ref, qn_ref, kn_ref, cos_ref, sin_ref,
                   q_ref, k_ref, v_ref, kb_ref, vb_ref, sg_ref, km_ref):
    xn = _rms(x_ref[...], ng_ref[...]).astype(BF16)
    w = WIDTH_ATT
    cos = cos_ref[...]
    sin = sin_ref[...]
    qf = _dot(xn, win_ref[:, 0:w])
    kf = _dot(xn, win_ref[:, w:2 * w])
    for h in range(N_HEADS):
        hs = slice(h * HEAD_DIM, (h + 1) * HEAD_DIM)
        q_ref[:, hs] = _rope(_head_rms(qf[:, hs], qn_ref[...]), cos, sin).astype(BF16)
        kh = _rope(_head_rms(kf[:, hs], kn_ref[...]), cos, sin)
        k_ref[:, hs] = kh
        kb_ref[:, hs] = kh.astype(BF16)
        km_ref[0, :, hs] = jnp.mean(kh, axis=0, keepdims=True)
    v = _dot(xn, win_ref[:, 2 * w:3 * w])
    v_ref[...] = v
    vb_ref[...] = v.astype(BF16)
    sg_ref[...] = _silu(_dot(xn, win_ref[:, 3 * w:4 * w]))


def _proj_c(x, wts, cos, sin, *, rows):
    n = x.shape[0]
    tiles = n // rows
    ng, win, qn, kn = wts
    consts_a = [ng, win, qn, kn]
    row = lambda i: (i, 0)
    wide = jax.ShapeDtypeStruct((n, WIDTH_ATT), F32)
    wide_b = jax.ShapeDtypeStruct((n, WIDTH_ATT), BF16)
    spec = pl.BlockSpec((rows, WIDTH_ATT), row)
    pos_tiles = cos.shape[0] // rows
    pos = lambda i: (i % pos_tiles, 0)
    return pl.pallas_call(
        _proj_c_kernel,
        out_shape=(wide_b, wide, wide, wide_b, wide_b, wide, jax.ShapeDtypeStruct((tiles, 1, WIDTH_ATT), F32)),
        grid=(tiles,),
        in_specs=[pl.BlockSpec((rows, D_MODEL), row)] + [_const_spec(c.shape) for c in consts_a]
        + [pl.BlockSpec((rows, HEAD_DIM), pos), pl.BlockSpec((rows, HEAD_DIM), pos)],
        out_specs=(spec, spec, spec, spec, spec, spec, pl.BlockSpec((1, 1, WIDTH_ATT), lambda i: (i, 0, 0))),
        compiler_params=_cparams("arbitrary"),
        name="proj_c",
    )(x, *consts_a, cos, sin)


def _proj_d_kernel(x_ref, ng_ref, win_ref, wf_ref, bf_ref, qn_ref, kn_ref,
                   q_ref, k_ref, v_ref, kb_ref, vb_ref, sg_ref, lf_ref, aux_ref, *rest, with_cum):
    xn = _rms(x_ref[...], ng_ref[...]).astype(BF16)
    rows = xn.shape[0]
    w = WIDTH_ATT
    qf = _dot(xn, win_ref[:, 0:w])
    kf = _dot(xn, win_ref[:, w:2 * w])
    for h in range(N_HEADS):
        hs = slice(h * HEAD_DIM, (h + 1) * HEAD_DIM)
        q_ref[:, hs] = _head_rms(qf[:, hs], qn_ref[...]).astype(BF16)
        kh = _head_rms(kf[:, hs], kn_ref[...])
        k_ref[:, hs] = kh
        kb_ref[:, hs] = kh.astype(BF16)
    v = _dot(xn, win_ref[:, 2 * w:3 * w])
    v_ref[...] = v
    vb_ref[...] = v.astype(BF16)
    sg_ref[...] = _silu(_dot(xn, win_ref[:, 3 * w:4 * w]))
    lane = lax.broadcasted_iota(jnp.int32, (rows, LANES), 1)
    logf = jnp.where(lane < N_HEADS, _log_sigmoid(_dot(xn, wf_ref[...]) + bf_ref[...]), 0.0)
    lf_ref[...] = logf[:, 0:N_HEADS]
    if with_cum:
        cumt_ref, carry_ref = rest

        @pl.when(pl.program_id(1) == 0)
        def _():
            carry_ref[...] = jnp.zeros((1, LANES), F32)

        r = lax.broadcasted_iota(jnp.int32, (rows, rows), 0)
        c = lax.broadcasted_iota(jnp.int32, (rows, rows), 1)
        tri = jnp.where(c <= r, 1.0, 0.0).astype(BF16)
        hi, mid, lo = _split3(logf)
        cum = _dot(tri, hi) + _dot(tri, mid) + _dot(tri, lo) + carry_ref[...]
        carry_ref[...] = cum[rows - 1:rows, :]
        aux_ref[...] = cum
        cumt_ref[0, 0] = jnp.transpose(cum)[0:N_HEADS, :]
    else:
        aux_ref[...] = logf


def _proj_d(x, wts, *, batch, seq, rows):
    n = x.shape[0]
    tiles = seq // rows if seq > 1 else 1
    nb = batch if seq > 1 else 1
    with_cum = seq > 1
    consts = list(wts)
    row = lambda b, t: (b * tiles + t, 0)
    wide = jax.ShapeDtypeStruct((n, WIDTH_ATT), F32)
    wide_b = jax.ShapeDtypeStruct((n, WIDTH_ATT), BF16)
    spec = pl.BlockSpec((rows, WIDTH_ATT), row)
    out_shape = [wide_b, wide, wide, wide_b, wide_b, wide,
                 jax.ShapeDtypeStruct((n, N_HEADS), F32), jax.ShapeDtypeStruct((n, LANES), F32)]
    out_specs = [spec] * 6 + [pl.BlockSpec((rows, N_HEADS), row), pl.BlockSpec((rows, LANES), row)]
    scratch = []
    if with_cum:
        out_shape.append(jax.ShapeDtypeStruct((nb, tiles, N_HEADS, rows), F32))
        out_specs.append(pl.BlockSpec((1, 1, N_HEADS, rows), lambda b, t: (b, t, 0, 0)))
        scratch = [pltpu.VMEM((1, LANES), F32)]
    return pl.pallas_call(
        functools.partial(_proj_d_kernel, with_cum=with_cum),
        out_shape=tuple(out_shape),
        grid=(nb, tiles),
        in_specs=[pl.BlockSpec((rows, D_MODEL), row)] + [_const_spec(c.shape) for c in consts],
        out_specs=tuple(out_specs),
        scratch_shapes=scratch,
        compiler_params=_cparams("arbitrary", "arbitrary"),
        name="proj_d_prompt" if with_cum else "proj_d_sample",
    )(x, *consts)


def _flash_step(qh, kj, vj, s_bias, mask, m, l, acc):
    s = _dot_nt(qh, kj) * SCALE
    if s_bias is not None:
        s = s + s_bias
    s = jnp.where(mask, s, NEG)
    m_new = jnp.maximum(m, jnp.max(s, axis=-1, keepdims=True))
    alpha = jnp.exp(m - m_new)
    p = jnp.exp(s - m_new)
    l = alpha * l + jnp.sum(p, axis=-1, keepdims=True)
    acc = alpha * acc + _dot(p.astype(BF16), vj)
    return m_new, l, acc


def _attn_c_prompt_kernel(q_ref, kb_ref, vb_ref, km_ref, sg_ref, y_ref):
    i = pl.program_id(1)
    blk = MOBA_BLOCK
    lane = lax.broadcasted_iota(jnp.int32, (blk, LANES), 1)
    row = lax.broadcasted_iota(jnp.int32, (blk, blk), 0)
    col = lax.broadcasted_iota(jnp.int32, (blk, blk), 1)
    causal = col <= row
    nblk = km_ref.shape[0]
    for h in range(N_HEADS):
        hs = slice(h * HEAD_DIM, (h + 1) * HEAD_DIM)
        qh = q_ref[:, hs]
        km = jnp.concatenate([km_ref[:, hs], jnp.zeros((LANES - nblk, HEAD_DIM), F32)], axis=0)
        km_hi = km.astype(BF16)
        km_lo = (km - km_hi.astype(F32)).astype(BF16)
        gate = _dot_nt(qh, km_hi) + _dot_nt(qh, km_lo)
        valid = lane < i
        selmat = jnp.zeros((blk, LANES), F32)
        for n in range(nblk):
            gn = gate[:, n:n + 1]
            ahead = valid & ((gate > gn) | ((gate == gn) & (lane < n)))
            rank = jnp.sum(jnp.where(ahead, 1.0, 0.0), axis=-1, keepdims=True)
            selmat = jnp.where((lane == n) & (rank < MOBA_TOPK) & valid, 1.0, selmat)

        def body(j, carry):
            m, l, acc = carry
            r0 = pl.multiple_of(j * blk, blk)
            selj = jnp.max(jnp.where(lane == j, selmat, 0.0), axis=-1, keepdims=True) > 0.0
            return _flash_step(qh, kb_ref[pl.ds(r0, blk), hs], vb_ref[pl.ds(r0, blk), hs], None, selj, m, l, acc)

        init = (jnp.full((blk, 1), NEG, F32), jnp.zeros((blk, 1), F32), jnp.zeros((blk, HEAD_DIM), F32))
        m, l, acc = lax.fori_loop(0, i, body, init)
        r0 = pl.multiple_of(i * blk, blk)
        m, l, acc = _flash_step(qh, kb_ref[pl.ds(r0, blk), hs], vb_ref[pl.ds(r0, blk), hs], None, causal, m, l, acc)
        y_ref[:, hs] = ((acc / l) * sg_ref[:, hs]).astype(BF16)


def _attn_c_prompt(q, kb, vb, km, sg, *, batch, seq):
    n = q.shape[0]
    tiles = seq // MOBA_BLOCK
    row = lambda b, t: (b * tiles + t, 0)
    per_b = lambda b, t: (b, 0)
    return pl.pallas_call(
        _attn_c_prompt_kernel,
        out_shape=jax.ShapeDtypeStruct((n, WIDTH_ATT), BF16),
        grid=(batch, tiles),
        in_specs=[pl.BlockSpec((MOBA_BLOCK, WIDTH_ATT), row), pl.BlockSpec((seq, WIDTH_ATT), per_b),
                  pl.BlockSpec((seq, WIDTH_ATT), per_b), pl.BlockSpec((tiles, WIDTH_ATT), per_b),
                  pl.BlockSpec((MOBA_BLOCK, WIDTH_ATT), row)],
        out_specs=pl.BlockSpec((MOBA_BLOCK, WIDTH_ATT), row),
        compiler_params=_cparams("arbitrary", "arbitrary"),
        name="attn_c_prompt",
    )(q, kb, vb, km, sg)


def _attn_d_prompt_kernel(q_ref, kb_ref, vb_ref, cum_ref, cumt_ref, sg_ref, y_ref):
    i = pl.program_id(1)
    blk = TM
    row = lax.broadcasted_iota(jnp.int32, (blk, blk), 0)
    col = lax.broadcasted_iota(jnp.int32, (blk, blk), 1)
    causal = col <= row
    for h in range(N_HEADS):
        hs = slice(h * HEAD_DIM, (h + 1) * HEAD_DIM)
        qh = q_ref[:, hs]
        cq = cum_ref[:, h:h + 1]

        def body(j, carry):
            m, l, acc = carry
            r0 = pl.multiple_of(j * blk, blk)
            bias = cq - cumt_ref[0, j, h:h + 1, :]
            return _flash_step(qh, kb_ref[pl.ds(r0, blk), hs], vb_ref[pl.ds(r0, blk), hs], bias, True, m, l, acc)

        init = (jnp.full((blk, 1), NEG, F32), jnp.zeros((blk, 1), F32), jnp.zeros((blk, HEAD_DIM), F32))
        m, l, acc = lax.fori_loop(0, i, body, init)
        r0 = pl.multiple_of(i * blk, blk)
        bias = cq - cumt_ref[0, i, h:h + 1, :]
        m, l, acc = _flash_step(qh, kb_ref[pl.ds(r0, blk), hs], vb_ref[pl.ds(r0, blk), hs], bias, causal, m, l, acc)
        y_ref[:, hs] = ((acc / l) * sg_ref[:, hs]).astype(BF16)


def _attn_d_prompt(q, kb, vb, cum, cumt, sg, *, batch, seq):
    n = q.shape[0]
    tiles = seq // TM
    row = lambda b, t: (b * tiles + t, 0)
    per_b = lambda b, t: (b, 0)
    return pl.pallas_call(
        _attn_d_prompt_kernel,
        out_shape=jax.ShapeDtypeStruct((n, WIDTH_ATT), BF16),
        grid=(batch, tiles),
        in_specs=[pl.BlockSpec((TM, WIDTH_ATT), row), pl.BlockSpec((seq, WIDTH_ATT), per_b),
                  pl.BlockSpec((seq, WIDTH_ATT), per_b), pl.BlockSpec((TM, LANES), row),
                  pl.BlockSpec((1, tiles, N_HEADS, TM), lambda b, t: (b, 0, 0, 0)),
                  pl.BlockSpec((TM, WIDTH_ATT), row)],
        out_specs=pl.BlockSpec((TM, WIDTH_ATT), row),
        compiler_params=_cparams("arbitrary", "arbitrary"),
        name="attn_d_prompt",
    )(q, kb, vb, cum, cumt, sg)


def _per_head_sum(x):
    parts = []
    for h in range(N_HEADS):
        s = jnp.sum(x[:, h * HEAD_DIM:(h + 1) * HEAD_DIM], axis=-1, keepdims=True)
        parts.append(jnp.broadcast_to(s, (1, HEAD_DIM)))
    return jnp.concatenate(parts, axis=-1)


def _attn_c_sample_kernel(pt_ref, q_ref, kn_ref, vn_ref, sg_ref, kp_ref, vp_ref, y_ref,
                          ksum_ref, m_ref, l_ref, acc_ref):
    p = pl.program_id(1)
    npages = pl.num_programs(1)
    q = q_ref[0].astype(F32)
    kp = kp_ref[0]
    vp = vp_ref[0]
    ksum_ref[p] = jnp.sum(kp, axis=0, keepdims=True)
    prod = kp * q
    ms, ls, accs = [], [], []
    for h in range(N_HEADS):
        hs = slice(h * HEAD_DIM, (h + 1) * HEAD_DIM)
        s = jnp.sum(prod[:, hs], axis=-1, keepdims=True) * SCALE
        mh = jnp.max(s, axis=0, keepdims=True)
        e = jnp.exp(s - mh)
        ms.append(jnp.broadcast_to(mh, (1, HEAD_DIM)))
        ls.append(jnp.broadcast_to(jnp.sum(e, axis=0, keepdims=True), (1, HEAD_DIM)))
        accs.append(jnp.sum(e * vp[:, hs], axis=0, keepdims=True))
    m_ref[p] = jnp.concatenate(ms, axis=-1)
    l_ref[p] = jnp.concatenate(ls, axis=-1)
    acc_ref[p] = jnp.concatenate(accs, axis=-1)

    @pl.when(p == npages - 1)
    def _():
        pages_per_blk = MOBA_BLOCK // PAGE_SIZE
        nblk = ksum_ref.shape[0] // pages_per_blk
        gates = []
        for n in range(nblk):
            ks = ksum_ref[pages_per_blk * n]
            for r in range(1, pages_per_blk):
                ks = ks + ksum_ref[pages_per_blk * n + r]
            gates.append(_per_head_sum(q * (ks * (1.0 / MOBA_BLOCK))))
        sels = []
        for n in range(nblk):
            rank = jnp.zeros((1, WIDTH_ATT), F32)
            for mth in range(nblk):
                if mth == n:
                    continue
                ahead = (gates[mth] > gates[n]) | (gates[mth] == gates[n]) if mth < n else gates[mth] > gates[n]
                rank = rank + jnp.where(ahead, 1.0, 0.0)
            sels.append(rank < MOBA_TOPK)
        kn = kn_ref[0]
        vn = vn_ref[0]
        s_self = _per_head_sum(q * kn) * SCALE
        mx = s_self
        for pg in range(ksum_ref.shape[0]):
            mx = jnp.maximum(mx, jnp.where(sels[pg // pages_per_blk], m_ref[pg], NEG))
        wself = jnp.exp(s_self - mx)
        l = wself
        acc = wself * vn
        for pg in range(ksum_ref.shape[0]):
            wgt = jnp.where(sels[pg // pages_per_blk], jnp.exp(m_ref[pg] - mx), 0.0)
            l = l + wgt * l_ref[pg]
            acc = acc + wgt * acc_ref[pg]
        y_ref[0] = ((acc / l) * sg_ref[0]).astype(BF16)


def _attn_c_sample(q, k_new, v_new, sg, cache_k, cache_v, page_table):
    n = q.shape[0]
    npages = page_table.shape[1]
    pool = cache_k.shape[0]
    ck = cache_k.reshape(pool, PAGE_SIZE, WIDTH_ATT)
    cv = cache_v.reshape(pool, PAGE_SIZE, WIDTH_ATT)
    r3 = lambda a: a.reshape(n, 1, WIDTH_ATT)
    rowspec = pl.BlockSpec((1, 1, WIDTH_ATT), lambda b, p, pt: (b, 0, 0))
    pagespec = pl.BlockSpec((1, PAGE_SIZE, WIDTH_ATT), lambda b, p, pt: (pt[b * npages + p], 0, 0))
    stat = pltpu.VMEM((npages, 1, WIDTH_ATT), F32)
    y = pl.pallas_call(
        _attn_c_sample_kernel,
        out_shape=jax.ShapeDtypeStruct((n, 1, WIDTH_ATT), BF16),
        grid_spec=pltpu.PrefetchScalarGridSpec(
            num_scalar_prefetch=1, grid=(n, npages),
            in_specs=[rowspec, rowspec, rowspec, rowspec, pagespec, pagespec],
            out_specs=rowspec,
            scratch_shapes=[stat, stat, stat, stat]),
        compiler_params=_cparams("arbitrary", "arbitrary"),
        name="attn_c_sample",
    )(page_table.reshape(-1), r3(q), r3(k_new), r3(v_new), r3(sg), ck, cv)
    return y.reshape(n, WIDTH_ATT)


def _attn_d_sample_kernel(pt_ref, q_ref, kn_ref, vn_ref, lfn_ref, sg_ref, kp_ref, vp_ref, lfp_ref, y_ref,
                          pad_ref, carry_ref, m_ref, l_ref, acc_ref):
    pp = pl.program_id(1)
    q = q_ref[0].astype(F32)

    @pl.when(pp == 0)
    def _():
        pad_ref[...] = jnp.zeros((PAGE_SIZE, LANES), F32)
        carry_ref[...] = lfn_ref[0]
        m_ref[...] = _per_head_sum(q * kn_ref[0]) * SCALE
        l_ref[...] = jnp.ones((1, WIDTH_ATT), F32)
        acc_ref[...] = vn_ref[0]

    pad_ref[:, 0:N_HEADS] = lfp_ref[0]
    lfp = pad_ref[...]
    r = lax.broadcasted_iota(jnp.int32, (PAGE_SIZE, PAGE_SIZE), 0)
    c = lax.broadcasted_iota(jnp.int32, (PAGE_SIZE, PAGE_SIZE), 1)
    upper = jnp.where(c > r, 1.0, 0.0).astype(BF16)
    hi, mid, lo = _split3(lfp)
    dec = _dot(upper, hi) + _dot(upper, mid) + _dot(upper, lo) + carry_ref[...]
    carry_ref[...] = dec[0:1, :] + lfp[0:1, :]

    kp = kp_ref[0]
    vp = vp_ref[0]
    prod = kp * q
    for h in range(N_HEADS):
        hs = slice(h * HEAD_DIM, (h + 1) * HEAD_DIM)
        s = jnp.sum(prod[:, hs], axis=-1, keepdims=True) * SCALE + dec[:, h:h + 1]
        m_old = m_ref[:, hs]
        m_new = jnp.maximum(m_old, jnp.max(s, axis=0, keepdims=True))
        alpha = jnp.exp(m_old - m_new)
        e = jnp.exp(s - m_new[:, 0:1])
        m_ref[:, hs] = m_new
        l_ref[:, hs] = alpha * l_ref[:, hs] + jnp.sum(e, axis=0, keepdims=True)
        acc_ref[:, hs] = alpha * acc_ref[:, hs] + jnp.sum(e * vp[:, hs], axis=0, keepdims=True)

    @pl.when(pp == pl.num_programs(1) - 1)
    def _():
        y_ref[0] = ((acc_ref[...] / l_ref[...]) * sg_ref[0]).astype(BF16)


def _attn_d_sample(q, k_new, v_new, lf_new, sg, cache_k, cache_v, cache_lf, page_table):
    n = q.shape[0]
    npages = page_table.shape[1]
    pool = cache_k.shape[0]
    ck = cache_k.reshape(pool, PAGE_SIZE, WIDTH_ATT)
    cv = cache_v.reshape(pool, PAGE_SIZE, WIDTH_ATT)
    r3 = lambda a: a.reshape(n, 1, a.shape[-1])
    rowspec = pl.BlockSpec((1, 1, WIDTH_ATT), lambda b, p, pt: (b, 0, 0))
    page = lambda b, p, pt: (pt[b * npages + (npages - 1 - p)], 0, 0)
    row = pltpu.VMEM((1, WIDTH_ATT), F32)
    y = pl.pallas_call(
        _attn_d_sample_kernel,
        out_shape=jax.ShapeDtypeStruct((n, 1, WIDTH_ATT), BF16),
        grid_spec=pltpu.PrefetchScalarGridSpec(
            num_scalar_prefetch=1, grid=(n, npages),
            in_specs=[rowspec, rowspec, rowspec, pl.BlockSpec((1, 1, LANES), lambda b, p, pt: (b, 0, 0)), rowspec,
                      pl.BlockSpec((1, PAGE_SIZE, WIDTH_ATT), page), pl.BlockSpec((1, PAGE_SIZE, WIDTH_ATT), page),
                      pl.BlockSpec((1, PAGE_SIZE, N_HEADS), page)],
            out_specs=rowspec,
            scratch_shapes=[pltpu.VMEM((PAGE_SIZE, LANES), F32), pltpu.VMEM((1, LANES), F32), row, row, row]),
        compiler_params=_cparams("arbitrary", "arbitrary"),
        name="attn_d_sample",
    )(page_table.reshape(-1), r3(q), r3(k_new), r3(v_new), r3(lf_new), r3(sg), ck, cv, cache_lf)
    return y.reshape(n, WIDTH_ATT)


def _out_pe_kernel(x_ref, y_ref, p_ref, wout_ref, peg_ref, wpg_ref, wpp_ref, xo_ref):
    x1 = x_ref[...] + _dot(y_ref[...], wout_ref[...])
    xo_ref[...] = _pe_update(x1, p_ref, peg_ref, wpg_ref, wpp_ref)


def _out_pe(x, y, p, wts, *, rows):
    n = x.shape[0]
    consts = list(wts)
    row = lambda i: (i, 0)
    return pl.pallas_call(
        _out_pe_kernel,
        out_shape=jax.ShapeDtypeStruct((n, D_MODEL), F32),
        grid=(n // rows,),
        in_specs=[pl.BlockSpec((rows, D_MODEL), row), pl.BlockSpec((rows, WIDTH_ATT), row),
                  pl.BlockSpec((rows, PLE_DIM), row)] + [_const_spec(c.shape) for c in consts],
        out_specs=pl.BlockSpec((rows, D_MODEL), row),
        compiler_params=_cparams("arbitrary"),
        name="out_pe",
    )(x, y, p, *consts)


def _rope_tables(pos):
    half = ROPE_DIM // 2
    inv = ROPE_THETA ** (-jnp.arange(half, dtype=F32) * 2.0 / ROPE_DIM)
    ang = pos.astype(F32)[:, None] * inv[None, :]
    cos, sin = jnp.cos(ang), jnp.sin(ang)
    rest = HEAD_DIM - ROPE_DIM
    ones = jnp.ones((pos.shape[0], rest), F32)
    zeros = jnp.zeros((pos.shape[0], rest), F32)
    return jnp.concatenate([cos, cos, ones], axis=-1), jnp.concatenate([-sin, sin, zeros], axis=-1)


def kernel(x_prompt, x_sample, state_conv_b, cache_k_c, cache_v_c, cache_k_d, cache_v_d, cache_logf_d, page_table, p_prompt, p_sample, norm_a, w_in_a, ln_g_a, ln_b_a, w_s_a, b_s_a, w_out_a, norm_b, w_in_b, conv_w_b, conv_b_b, ln_g_b, ln_b_b, w_out_b, norm_c, w_in_c, qn_c, kn_c, w_out_c, norm_d, w_in_d, b_f_d, qn_d, kn_d, w_out_d, pe_norm, w_pe_gate, w_pe_proj):
    bsz, seq, _ = x_prompt.shape
    nsmp = x_sample.shape[0]
    depth = pe_norm.shape[0]
    past_len = page_table.shape[1] * PAGE_SIZE
    row2 = lambda a: a.reshape(1, -1)

    xp = x_prompt.reshape(bsz * seq, D_MODEL)
    xs = x_sample.reshape(nsmp, D_MODEL)
    cos_p, sin_p = _rope_tables(jnp.arange(seq))
    cos_s, sin_s = _rope_tables(jnp.full((nsmp,), past_len))

    outs = {k: [] for k in ("va_p", "va_s", "cb_p", "cb_s", "kc_p", "vc_p", "kc_s", "vc_s",
                            "kd_p", "vd_p", "lf_p", "kd_s", "vd_s", "lf_s")}
    for i in range(depth):
        j, kind = divmod(i, 4)
        pp = p_prompt[i].reshape(bsz * seq, PLE_DIM)
        ps = p_sample[i].reshape(nsmp, PLE_DIM)
        pe = (row2(pe_norm[i]), w_pe_gate[i].astype(BF16), w_pe_proj[i].astype(BF16))
        if kind == 0:
            gw = WIDTH_A // GROUPS_A
            base = (row2(norm_a[j]), w_in_a[j].astype(BF16), row2(ln_g_a[j]), row2(ln_b_a[j]))
            tail = (w_out_a[j].astype(BF16),) + pe
            ws_p = jnp.tril(w_s_a[j]).astype(BF16)
            bs_p = jnp.repeat(jnp.transpose(b_s_a[j]), gw, axis=1)
            xp, va = _layer_a(xp, pp, base + (ws_p, bs_p) + tail, batch=bsz, seq=seq)
            ws_s = row2(jnp.repeat(w_s_a[j][:, 0, 0], gw))
            bs_s = row2(jnp.repeat(b_s_a[j][:, 0], gw))
            xs, vs = _layer_a(xs, ps, base + (ws_s, bs_s) + tail, batch=nsmp, seq=1)
            outs["va_p"].append(va)
            outs["va_s"].append(vs.reshape(nsmp, 1, WIDTH_A))
        elif kind == 1:
            wts = (row2(norm_b[j]), w_in_b[j].astype(BF16), conv_w_b[j], row2(conv_b_b[j]),
                   row2(ln_g_b[j]), row2(ln_b_b[j]), w_out_b[j].astype(BF16)) + pe
            xp, cb = _layer_b_prompt(xp, pp, wts, batch=bsz, seq=seq)
            xs, hs = _layer_b_sample(xs, ps, state_conv_b[j], wts)
            outs["cb_p"].append(cb)
            outs["cb_s"].append(jnp.concatenate([state_conv_b[j][:, 1:], hs[:, None, :]], axis=1))
        elif kind == 2:
            wts = (row2(norm_c[j]), w_in_c[j].astype(BF16), row2(qn_c[j]), row2(kn_c[j]))
            tail = (w_out_c[j].astype(BF16),) + pe
            q, k, v, kb, vb, sg, km = _proj_c(xp, wts, cos_p, sin_p, rows=TM)
            y = _attn_c_prompt(q, kb, vb, km.reshape(-1, WIDTH_ATT), sg, batch=bsz, seq=seq)
            xp = _out_pe(xp, y, pp, tail, rows=TM)
            outs["kc_p"].append(k.reshape(bsz, seq, N_HEADS, HEAD_DIM))
            outs["vc_p"].append(v.reshape(bsz, seq, N_HEADS, HEAD_DIM))
            q, k, v, _, _, sg, _ = _proj_c(xs, wts, cos_s, sin_s, rows=nsmp)
            y = _attn_c_sample(q, k, v, sg, cache_k_c[j], cache_v_c[j], page_table)
            xs = _out_pe(xs, y, ps, tail, rows=nsmp)
            outs["kc_s"].append(k.reshape(nsmp, 1, N_HEADS, HEAD_DIM))
            outs["vc_s"].append(v.reshape(nsmp, 1, N_HEADS, HEAD_DIM))
        else:
            wf = jnp.pad(w_in_d[j][:, 4 * WIDTH_ATT:], ((0, 0), (0, LANES - N_HEADS))).astype(BF16)
            bf = jnp.pad(b_f_d[j], (0, LANES - N_HEADS)).reshape(1, LANES)
            wts = (row2(norm_d[j]), w_in_d[j][:, :4 * WIDTH_ATT].astype(BF16), wf, bf, row2(qn_d[j]), row2(kn_d[j]))
            tail = (w_out_d[j].astype(BF16),) + pe
            q, k, v, kb, vb, sg, lf, cum, cumt = _proj_d(xp, wts, batch=bsz, seq=seq, rows=TM)
            y = _attn_d_prompt(q, kb, vb, cum, cumt, sg, batch=bsz, seq=seq)
            xp = _out_pe(xp, y, pp, tail, rows=TM)
            outs["kd_p"].append(k.reshape(bsz, seq, N_HEADS, HEAD_DIM))
            outs["vd_p"].append(v.reshape(bsz, seq, N_HEADS, HEAD_DIM))
            outs["lf_p"].append(lf.reshape(bsz, seq, N_HEADS))
            q, k, v, _, _, sg, lf, lfpad = _proj_d(xs, wts, batch=nsmp, seq=1, rows=nsmp)
            y = _attn_d_sample(q, k, v, lfpad, sg, cache_k_d[j], cache_v_d[j], cache_logf_d[j], page_table)
            xs = _out_pe(xs, y, ps, tail, rows=nsmp)
            outs["kd_s"].append(k.reshape(nsmp, 1, N_HEADS, HEAD_DIM))
            outs["vd_s"].append(v.reshape(nsmp, 1, N_HEADS, HEAD_DIM))
            outs["lf_s"].append(lf.reshape(nsmp, 1, N_HEADS))

    st = lambda key: jnp.stack(outs[key])
    return (xp.reshape(bsz, seq, D_MODEL), xs.reshape(nsmp, 1, D_MODEL),
            st("va_p"), st("va_s"), st("cb_p"), st("cb_s"),
            st("kc_p"), st("vc_p"), st("kc_s"), st("vc_s"),
            st("kd_p"), st("vd_p"), st("lf_p"), st("kd_s"), st("vd_s"), st("lf_s"))
```

```python
import functools
import math

import jax
import jax.numpy as jnp
import numpy as np
from jax import lax
from jax.experimental import pallas as pl
from jax.experimental.pallas import tpu as pltpu

F32 = jnp.float32
BF16 = jnp.bfloat16

D_MODEL = 1024
PLE_DIM = 256
WIDTH_A = 2048
GROUPS_A = 8
CHUNK_A = 128
WIDTH_B = 2048
CONV_W = 31
N_HEADS = 8
HEAD_DIM = 128
WIDTH_ATT = N_HEADS * HEAD_DIM
ROPE_DIM = HEAD_DIM // 4
ROPE_THETA = 500000.0
MOBA_BLOCK = 256
MOBA_TOPK = 3
PAGE_SIZE = 128
EPS = 1e-6

LANES = 128
TM = 256
CONV_PAD = 32
NEG = -1e30
SCALE = HEAD_DIM ** -0.5
VMEM_LIMIT = 60 * 1024 * 1024


def _cparams(*sem):
    return pltpu.CompilerParams(dimension_semantics=sem, vmem_limit_bytes=VMEM_LIMIT)


def _const_spec(shape):
    nd = len(shape)
    return pl.BlockSpec(shape, lambda *_: (0,) * nd, pipeline_mode=pl.Buffered(1))


def _dot(a, b):
    return jnp.dot(a, b, preferred_element_type=F32)


def _dot_nt(a, b):
    return lax.dot_general(a, b, (((1,), (1,)), ((), ())), preferred_element_type=F32)


def _sigmoid(x):
    return 1.0 / (1.0 + jnp.exp(-x))


def _silu(x):
    return x * _sigmoid(x)


def _gelu(x):
    return 0.5 * x * (1.0 + jnp.tanh(math.sqrt(2.0 / math.pi) * (x + 0.044715 * (x * x * x))))


def _rms(x, g):
    return x * lax.rsqrt(jnp.mean(x * x, axis=-1, keepdims=True) + EPS) * g


def _layer_norm(x, g, b):
    xc = x - jnp.mean(x, axis=-1, keepdims=True)
    return xc * lax.rsqrt(jnp.mean(xc * xc, axis=-1, keepdims=True) + EPS) * g + b


def _log_sigmoid(x):
    return jnp.minimum(x, 0.0) - jnp.log(1.0 + jnp.exp(-jnp.abs(x)))


def _split3(x):
    hi = x.astype(BF16)
    r = x - hi.astype(F32)
    mid = r.astype(BF16)
    lo = (r - mid.astype(F32)).astype(BF16)
    return hi, mid, lo


def _pe_update(x1, p_ref, peg_ref, wpg_ref, wpp_ref):
    xn = _rms(x1, peg_ref[...]).astype(BF16)
    gate = _sigmoid(_dot(xn, wpg_ref[...]))
    return x1 + gate * _dot(p_ref[...].astype(BF16), wpp_ref[...])


def _layer_a_kernel(x_ref, p_ref, ng_ref, win_ref, lng_ref, lnb_ref, ws_ref, bs_ref, wout_ref,
                    peg_ref, wpg_ref, wpp_ref, xo_ref, vo_ref, *scratch, chunked):
    x = x_ref[...]
    rows = x.shape[0]
    xn = _rms(x, ng_ref[...]).astype(BF16)
    w = WIDTH_A
    u = _gelu(_dot(xn, win_ref[:, 0:w]))
    v = _layer_norm(_gelu(_dot(xn, win_ref[:, w:2 * w])), lng_ref[...], lnb_ref[...])
    sg = _silu(_dot(xn, win_ref[:, 2 * w:3 * w]))
    if chunked:
        (s_ref,) = scratch
        vo_ref[0] = v[rows - CHUNK_A:, :]
        vb = v.astype(BF16)
        gw = WIDTH_A // GROUPS_A
        for c in range(rows // CHUNK_A):
            rs = slice(c * CHUNK_A, (c + 1) * CHUNK_A)
            for g in range(GROUPS_A):
                cs = slice(g * gw, (g + 1) * gw)
                s_ref[rs, cs] = _dot(ws_ref[g], vb[rs, cs]) + bs_ref[:, cs]
        s = s_ref[...]
    else:
        vo_ref[...] = v
        s = v * ws_ref[...] + bs_ref[...]
    y = (u * s * sg).astype(BF16)
    x1 = x + _dot(y, wout_ref[...])
    xo_ref[...] = _pe_update(x1, p_ref, peg_ref, wpg_ref, wpp_ref)


def _layer_a(x, p, wts, *, batch, seq):
    n = x.shape[0]
    chunked = seq > 1
    ng, win, lng, lnb, ws, bs, wout, peg, wpg, wpp = wts
    if chunked:
        tiles = seq // TM
        grid = (batch, tiles)
        row = lambda b, t: (b * tiles + t, 0)
        rows = TM
        vo_shape = jax.ShapeDtypeStruct((batch, CHUNK_A, WIDTH_A), F32)
        vo_spec = pl.BlockSpec((1, CHUNK_A, WIDTH_A), lambda b, t: (b, 0, 0))
        scratch = [pltpu.VMEM((TM, WIDTH_A), F32)]
    else:
        grid = (1, 1)
        row = lambda b, t: (0, 0)
        rows = n
        vo_shape = jax.ShapeDtypeStruct((n, WIDTH_A), F32)
        vo_spec = pl.BlockSpec((n, WIDTH_A), row)
        scratch = []
    consts = [ng, win, lng, lnb, ws, bs, wout, peg, wpg, wpp]
    return pl.pallas_call(
        functools.partial(_layer_a_kernel, chunked=chunked),
        out_shape=(jax.ShapeDtypeStruct((n, D_MODEL), F32), vo_shape),
        grid=grid,
        in_specs=[pl.BlockSpec((rows, D_MODEL), row), pl.BlockSpec((rows, PLE_DIM), row)]
        + [_const_spec(c.shape) for c in consts],
        out_specs=(pl.BlockSpec((rows, D_MODEL), row), vo_spec),
        scratch_shapes=scratch,
        compiler_params=_cparams("arbitrary", "arbitrary"),
        name="layer_a_prompt" if chunked else "layer_a_sample",
    )(x, p, *consts)


CONV_ROWS = 64


def _layer_b_prompt_kernel(x_ref, p_ref, ng_ref, win_ref, cw_ref, cb_ref, lng_ref, lnb_ref, wout_ref,
                           peg_ref, wpg_ref, wpp_ref, xo_ref, st_ref, hext_ref, c_ref):
    t = pl.program_id(1)
    ncb = WIDTH_B // LANES

    @pl.when(t == 0)
    def _():
        hext_ref[:, 0:CONV_PAD, :] = jnp.zeros((ncb, CONV_PAD, LANES), F32)

    x = x_ref[...]
    xn = _rms(x, ng_ref[...]).astype(BF16)
    w = WIDTH_B
    h = _dot(xn, win_ref[:, 0:w]) * _sigmoid(_dot(xn, win_ref[:, w:2 * w]))
    sg = _silu(_dot(xn, win_ref[:, 2 * w:3 * w]))
    for cb in range(ncb):
        hext_ref[cb, CONV_PAD:CONV_PAD + TM, :] = h[:, cb * LANES:(cb + 1) * LANES]
    st_ref[0] = h[TM - (CONV_W - 1):, :]

    base = CONV_PAD - (CONV_W - 1)

    def col_body(cb, carry):
        for rg in range(TM // CONV_ROWS):
            accs = [None] * (CONV_ROWS // 8)
            for k in range(CONV_W):
                wk = cw_ref[cb, k:k + 1, :]
                for j in range(CONV_ROWS // 8):
                    r0 = rg * CONV_ROWS + j * 8 + base + k
                    term = wk * hext_ref[cb, r0:r0 + 8, :]
                    accs[j] = term if accs[j] is None else accs[j] + term
            for j in range(CONV_ROWS // 8):
                r0 = rg * CONV_ROWS + j * 8
                c_ref[cb, r0:r0 + 8, :] = accs[j]
        return carry

    lax.fori_loop(0, ncb, col_body, 0)

    for cb in range(ncb):
        hext_ref[cb, 0:CONV_PAD, :] = hext_ref[cb, TM:TM + CONV_PAD, :]
    c = jnp.concatenate([c_ref[cb] for cb in range(ncb)], axis=-1) + cb_ref[...]
    c = _silu(_layer_norm(c, lng_ref[...], lnb_ref[...]))
    y = (c * sg).astype(BF16)
    x1 = x + _dot(y, wout_ref[...])
    xo_ref[...] = _pe_update(x1, p_ref, peg_ref, wpg_ref, wpp_ref)


def _layer_b_prompt(x, p, wts, *, batch, seq):
    n = x.shape[0]
    tiles = seq // TM
    ng, win, cw, cb, lng, lnb, wout, peg, wpg, wpp = wts
    ncb = WIDTH_B // LANES
    cw3 = jnp.transpose(cw.reshape(CONV_W, ncb, LANES), (1, 0, 2))
    consts = [ng, win, cw3, cb, lng, lnb, wout, peg, wpg, wpp]
    row = lambda b, t: (b * tiles + t, 0)
    return pl.pallas_call(
        _layer_b_prompt_kernel,
        out_shape=(jax.ShapeDtypeStruct((n, D_MODEL), F32),
                   jax.ShapeDtypeStruct((batch, CONV_W - 1, WIDTH_B), F32)),
        grid=(batch, tiles),
        in_specs=[pl.BlockSpec((TM, D_MODEL), row), pl.BlockSpec((TM, PLE_DIM), row)]
        + [_const_spec(c.shape) for c in consts],
        out_specs=(pl.BlockSpec((TM, D_MODEL), row),
                   pl.BlockSpec((1, CONV_W - 1, WIDTH_B), lambda b, t: (b, 0, 0))),
        scratch_shapes=[pltpu.VMEM((ncb, TM + CONV_PAD, LANES), F32), pltpu.VMEM((ncb, TM, LANES), F32)],
        compiler_params=_cparams("arbitrary", "arbitrary"),
        name="layer_b_prompt",
    )(x, p, *consts)


SAMPLE_CONV_ROWS = 16


def _layer_b_sample_kernel(x_ref, p_ref, st_ref, ng_ref, win_ref, cw_ref, cb_ref, lng_ref, lnb_ref, wout_ref,
                           peg_ref, wpg_ref, wpp_ref, xo_ref, ho_ref, h_ref, sg_ref, c_ref):
    i = pl.program_id(0)
    w = WIDTH_B

    @pl.when(i == 0)
    def _():
        xn = _rms(x_ref[...], ng_ref[...]).astype(BF16)
        h = _dot(xn, win_ref[:, 0:w]) * _sigmoid(_dot(xn, win_ref[:, w:2 * w]))
        h_ref[...] = h
        ho_ref[...] = h
        sg_ref[...] = _silu(_dot(xn, win_ref[:, 2 * w:3 * w]))

    r0 = pl.multiple_of(i * SAMPLE_CONV_ROWS, SAMPLE_CONV_ROWS)
    acc = cb_ref[...] + cw_ref[CONV_W - 1:CONV_W, :] * h_ref[pl.ds(r0, SAMPLE_CONV_ROWS), :]
    for k in range(CONV_W - 1):
        acc = acc + cw_ref[k:k + 1, :] * st_ref[:, k, :]
    c_ref[pl.ds(r0, SAMPLE_CONV_ROWS), :] = acc

    @pl.when(i == pl.num_programs(0) - 1)
    def _():
        c = _silu(_layer_norm(c_ref[...], lng_ref[...], lnb_ref[...]))
        y = (c * sg_ref[...]).astype(BF16)
        x1 = x_ref[...] + _dot(y, wout_ref[...])
        xo_ref[...] = _pe_update(x1, p_ref, peg_ref, wpg_ref, wpp_ref)


def _layer_b_sample(x, p, state, wts):
    n = x.shape[0]
    consts = list(wts)
    full = lambda i: (0, 0)
    return pl.pallas_call(
        _layer_b_sample_kernel,
        out_shape=(jax.ShapeDtypeStruct((n, D_MODEL), F32), jax.ShapeDtypeStruct((n, WIDTH_B), F32)),
        grid=(n // SAMPLE_CONV_ROWS,),
        in_specs=[pl.BlockSpec((n, D_MODEL), full), pl.BlockSpec((n, PLE_DIM), full),
                  pl.BlockSpec((SAMPLE_CONV_ROWS, CONV_W - 1, WIDTH_B), lambda i: (i, 0, 0))]
        + [_const_spec(c.shape) for c in consts],
        out_specs=(pl.BlockSpec((n, D_MODEL), full), pl.BlockSpec((n, WIDTH_B), full)),
        scratch_shapes=[pltpu.VMEM((n, WIDTH_B), F32)] * 3,
        compiler_params=_cparams("arbitrary"),
        name="layer_b_sample",
    )(x, p, state, *consts)


AUG = 2 * HEAD_DIM


def _head_rms(xh, g):
    return xh * lax.rsqrt(jnp.mean(xh * xh, axis=-1, keepdims=True) + EPS) * g


def _rope(xh, cos, sin):
    half = ROPE_DIM // 2
    lane = lax.broadcasted_iota(jnp.int32, xh.shape, 1)
    partner = jnp.where(lane < half, pltpu.roll(xh, LANES - half, 1), pltpu.roll(xh, half, 1))
    return xh * cos + partner * sin


def _proj_c_kernel(x_ref, ng_ref, win_ref, qn_ref, kn_ref, cos_ref, sin_ref,
                   q_ref, k_ref, v_ref, sg_ref, *rest, prompt):
    xn = _rms(x_ref[...], ng_ref[...]).astype(BF16)
    rows = xn.shape[0]
    w = WIDTH_ATT
    cos = cos_ref[...]
    sin = sin_ref[...]
    qf = _dot(xn, win_ref[:, 0:w])
    kf = _dot(xn, win_ref[:, w:2 * w])
    if prompt:
        vb_ref, km_ref, kt_ref = rest
        sub = lax.broadcasted_iota(jnp.int32, (HEAD_DIM, rows), 0)
        onehot = jnp.where(sub == pl.program_id(1), 1.0, 0.0).astype(BF16)
    for h in range(N_HEADS):
        hs = slice(h * HEAD_DIM, (h + 1) * HEAD_DIM)
        q_ref[:, hs] = _rope(_head_rms(qf[:, hs], qn_ref[...]), cos, sin).astype(BF16)
        kh = _rope(_head_rms(kf[:, hs], kn_ref[...]), cos, sin)
        k_ref[:, hs] = kh
        if prompt:
            km_ref[0, :, hs] = jnp.mean(kh, axis=0, keepdims=True)
            kt_ref[0, 0, h, 0:HEAD_DIM, :] = jnp.transpose(kh).astype(BF16)
            kt_ref[0, 0, h, HEAD_DIM:AUG, :] = onehot
    v = _dot(xn, win_ref[:, 2 * w:3 * w])
    v_ref[...] = v
    if prompt:
        vb_ref[...] = v.astype(BF16)
    sg_ref[...] = _silu(_dot(xn, win_ref[:, 3 * w:4 * w]))


def _proj_c(x, wts, cos, sin, *, batch, seq):
    n = x.shape[0]
    prompt = seq > 1
    rows = TM if prompt else n
    tiles = seq // rows if prompt else 1
    nb = batch if prompt else 1
    consts = list(wts)
    row = lambda b, t: (b * tiles + t, 0)
    pos = lambda b, t: (t, 0)
    wide = jax.ShapeDtypeStruct((n, WIDTH_ATT), F32)
    wide_b = jax.ShapeDtypeStruct((n, WIDTH_ATT), BF16)
    spec = pl.BlockSpec((rows, WIDTH_ATT), row)
    out_shape = [wide_b, wide, wide, wide]
    out_specs = [spec] * 4
    if prompt:
        out_shape += [wide_b, jax.ShapeDtypeStruct((nb * tiles, 1, WIDTH_ATT), F32),
                      jax.ShapeDtypeStruct((nb, tiles, N_HEADS, AUG, rows), BF16)]
        out_specs += [spec, pl.BlockSpec((1, 1, WIDTH_ATT), lambda b, t: (b * tiles + t, 0, 0)),
                      pl.BlockSpec((1, 1, N_HEADS, AUG, rows), lambda b, t: (b, t, 0, 0, 0))]
    return pl.pallas_call(
        functools.partial(_proj_c_kernel, prompt=prompt),
        out_shape=tuple(out_shape),
        grid=(nb, tiles),
        in_specs=[pl.BlockSpec((rows, D_MODEL), row)] + [_const_spec(c.shape) for c in consts]
        + [pl.BlockSpec((rows, HEAD_DIM), pos), pl.BlockSpec((rows, HEAD_DIM), pos)],
        out_specs=tuple(out_specs),
        compiler_params=_cparams("arbitrary", "arbitrary"),
        name="proj_c_prompt" if prompt else "proj_c_sample",
    )(x, *consts, cos, sin)


def _proj_d_kernel(x_ref, ng_ref, win_ref, wf_ref, bf_ref, wft_ref, bft_ref, qn_ref, kn_ref,
                   q_ref, k_ref, v_ref, sg_ref, lf_ref, *rest, prompt):
    xn = _rms(x_ref[...], ng_ref[...]).astype(BF16)
    rows = xn.shape[0]
    w = WIDTH_ATT
    qf = _dot(xn, win_ref[:, 0:w])
    kf = _dot(xn, win_ref[:, w:2 * w])
    v = _dot(xn, win_ref[:, 2 * w:3 * w])
    v_ref[...] = v
    sg_ref[...] = _silu(_dot(xn, win_ref[:, 3 * w:4 * w]))
    lane = lax.broadcasted_iota(jnp.int32, (rows, LANES), 1)
    logf = jnp.where(lane < N_HEADS, _log_sigmoid(_dot(xn, wf_ref[...]) + bf_ref[...]), 0.0)
    lf_ref[...] = logf[:, 0:N_HEADS]
    if not prompt:
        (lft_ref,) = rest
        for h in range(N_HEADS):
            hs = slice(h * HEAD_DIM, (h + 1) * HEAD_DIM)
            q_ref[:, hs] = _head_rms(qf[:, hs], qn_ref[...]).astype(BF16)
            k_ref[:, hs] = _head_rms(kf[:, hs], kn_ref[...])
        lft_ref[...] = _log_sigmoid(_dot_nt(wft_ref[...], xn) + bft_ref[...])
        return

    vb_ref, kt_ref, carry_ref = rest
    vb_ref[...] = v.astype(BF16)

    @pl.when(pl.program_id(1) == 0)
    def _():
        carry_ref[...] = jnp.zeros((1, LANES), F32)

    r = lax.broadcasted_iota(jnp.int32, (rows, rows), 0)
    c = lax.broadcasted_iota(jnp.int32, (rows, rows), 1)
    tri = jnp.where(c <= r, 1.0, 0.0).astype(BF16)
    hi, mid, lo = _split3(logf)
    cum = _dot(tri, hi) + _dot(tri, mid) + _dot(tri, lo) + carry_ref[...]
    carry_ref[...] = cum[rows - 1:rows, :]
    pieces = [p.astype(F32) for p in _split3(cum * (1.0 / SCALE))]
    pieces_t = [jnp.transpose(p) for p in pieces]
    sub = lax.broadcasted_iota(jnp.int32, (HEAD_DIM, rows), 0)
    for h in range(N_HEADS):
        hs = slice(h * HEAD_DIM, (h + 1) * HEAD_DIM)
        c1, c2, c3 = [p[:, h:h + 1] for p in pieces]
        aug_q = jnp.where(lane == 0, c1, jnp.where(lane == 1, c2, jnp.where(lane == 2, c3,
                          jnp.where(lane < 6, 1.0, 0.0))))
        q_ref[:, h * AUG:h * AUG + HEAD_DIM] = _head_rms(qf[:, hs], qn_ref[...]).astype(BF16)
        q_ref[:, h * AUG + HEAD_DIM:(h + 1) * AUG] = aug_q.astype(BF16)
        kh = _head_rms(kf[:, hs], kn_ref[...])
        k_ref[:, hs] = kh
        d1, d2, d3 = [p[h:h + 1, :] for p in pieces_t]
        aug_k = jnp.where(sub == 3, -d1, jnp.where(sub == 4, -d2, jnp.where(sub == 5, -d3,
                          jnp.where(sub < 3, 1.0, 0.0))))
        kt_ref[0, 0, h, 0:HEAD_DIM, :] = jnp.transpose(kh).astype(BF16)
        kt_ref[0, 0, h, HEAD_DIM:AUG, :] = aug_k.astype(BF16)


def _proj_d(x, wts, *, batch, seq):
    n = x.shape[0]
    prompt = seq > 1
    rows = TM if prompt else n
    tiles = seq // rows if prompt else 1
    nb = batch if prompt else 1
    consts = list(wts)
    row = lambda b, t: (b * tiles + t, 0)
    wide = jax.ShapeDtypeStruct((n, WIDTH_ATT), F32)
    spec = pl.BlockSpec((rows, WIDTH_ATT), row)
    qw = N_HEADS * AUG if prompt else WIDTH_ATT
    out_shape = [jax.ShapeDtypeStruct((n, qw), BF16), wide, wide, wide, jax.ShapeDtypeStruct((n, N_HEADS), F32)]
    out_specs = [pl.BlockSpec((rows, qw), row), spec, spec, spec, pl.BlockSpec((rows, N_HEADS), row)]
    scratch = []
    if prompt:
        out_shape += [jax.ShapeDtypeStruct((n, WIDTH_ATT), BF16),
                      jax.ShapeDtypeStruct((nb, tiles, N_HEADS, AUG, rows), BF16)]
        out_specs += [spec, pl.BlockSpec((1, 1, N_HEADS, AUG, rows), lambda b, t: (b, t, 0, 0, 0))]
        scratch = [pltpu.VMEM((1, LANES), F32)]
    else:
        out_shape.append(jax.ShapeDtypeStruct((LANES, n), F32))
        out_specs.append(pl.BlockSpec((LANES, n), lambda b, t: (0, 0)))
    return pl.pallas_call(
        functools.partial(_proj_d_kernel, prompt=prompt),
        out_shape=tuple(out_shape),
        grid=(nb, tiles),
        in_specs=[pl.BlockSpec((rows, D_MODEL), row)] + [_const_spec(c.shape) for c in consts],
        out_specs=tuple(out_specs),
        scratch_shapes=scratch,
        compiler_params=_cparams("arbitrary", "arbitrary"),
        name="proj_d_prompt" if prompt else "proj_d_sample",
    )(x, *consts)


EXP2_SCALE = SCALE * math.log2(math.e)


def _flash_block(j, qa, kt_ref, vb_ref, m_ref, l_ref, acc_ref, causal):
    blk = TM
    r0 = pl.multiple_of(j * blk, blk)
    for h in range(N_HEADS):
        hs = slice(h * HEAD_DIM, (h + 1) * HEAD_DIM)
        s = _dot(qa(h), kt_ref[0, j, h])
        if causal is not None:
            s = jnp.where(causal, s, NEG)
        s0, s1 = s[:, 0:LANES], s[:, LANES:2 * LANES]
        m_old = m_ref[h]
        m_new = jnp.maximum(m_old, jnp.max(jnp.maximum(s0, s1), axis=-1, keepdims=True))
        alpha = jnp.exp2((m_old - m_new) * EXP2_SCALE)
        p0 = jnp.exp2((s0 - m_new) * EXP2_SCALE)
        p1 = jnp.exp2((s1 - m_new) * EXP2_SCALE)
        l_ref[h] = alpha * l_ref[h] + jnp.sum(p0 + p1, axis=-1, keepdims=True)
        pv = _dot(jnp.concatenate([p0, p1], axis=-1).astype(BF16), vb_ref[pl.ds(r0, blk), hs])
        acc_ref[h] = alpha * acc_ref[h] + pv
        m_ref[h] = m_new


def _flash_all(i, qa, kt_ref, vb_ref, sg_ref, y_ref, m_ref, l_ref, acc_ref):
    blk = TM
    m_ref[...] = jnp.full(m_ref.shape, NEG, F32)
    l_ref[...] = jnp.zeros(l_ref.shape, F32)
    acc_ref[...] = jnp.zeros(acc_ref.shape, F32)

    def body(j, carry):
        _flash_block(j, qa, kt_ref, vb_ref, m_ref, l_ref, acc_ref, None)
        return carry

    lax.fori_loop(0, i, body, 0)
    row = lax.broadcasted_iota(jnp.int32, (blk, blk), 0)
    col = lax.broadcasted_iota(jnp.int32, (blk, blk), 1)
    _flash_block(i, qa, kt_ref, vb_ref, m_ref, l_ref, acc_ref, col <= row)
    for h in range(N_HEADS):
        hs = slice(h * HEAD_DIM, (h + 1) * HEAD_DIM)
        y_ref[:, hs] = ((acc_ref[h] / l_ref[h]) * sg_ref[:, hs]).astype(BF16)


def _attn_c_prompt_kernel(q_ref, kt_ref, vb_ref, km_ref, sg_ref, y_ref, qa_ref, m_ref, l_ref, acc_ref):
    i = pl.program_id(1)
    blk = MOBA_BLOCK
    nblk = km_ref.shape[0]
    sub = lax.broadcasted_iota(jnp.int32, (nblk, blk), 0)
    valid = sub < i
    for h in range(N_HEADS):
        hs = slice(h * HEAD_DIM, (h + 1) * HEAD_DIM)
        qh = q_ref[:, hs]
        km = km_ref[:, hs]
        km_hi = km.astype(BF16)
        km_lo = (km - km_hi.astype(F32)).astype(BF16)
        gate = _dot_nt(km_hi, qh) + _dot_nt(km_lo, qh)
        bias = jnp.zeros((nblk, blk), F32)
        for n in range(nblk):
            gn = gate[n:n + 1, :]
            ahead = valid & ((gate > gn) | ((gate == gn) & (sub < n)))
            rank = jnp.sum(jnp.where(ahead, 1.0, 0.0), axis=0, keepdims=True)
            bias = jnp.where((sub == n) & (rank >= MOBA_TOPK) & valid, NEG, bias)
        bias = jnp.concatenate([bias, jnp.zeros((LANES - nblk, blk), F32)], axis=0)
        qa_ref[h, :, 0:HEAD_DIM] = qh
        qa_ref[h, :, HEAD_DIM:AUG] = jnp.transpose(bias).astype(BF16)
    _flash_all(i, lambda h: qa_ref[h], kt_ref, vb_ref, sg_ref, y_ref, m_ref, l_ref, acc_ref)


def _attn_d_prompt_kernel(q_ref, kt_ref, vb_ref, sg_ref, y_ref, m_ref, l_ref, acc_ref):
    i = pl.program_id(1)
    _flash_all(i, lambda h: q_ref[:, h * AUG:(h + 1) * AUG], kt_ref, vb_ref, sg_ref, y_ref, m_ref, l_ref, acc_ref)


def _attn_prompt(q, kt, vb, sg, km, *, batch, seq):
    n = q.shape[0]
    tiles = seq // TM
    row = lambda b, t: (b * tiles + t, 0)
    per_b = lambda b, t: (b, 0)
    stat = pltpu.VMEM((N_HEADS, TM, LANES), F32)
    in_specs = [pl.BlockSpec((TM, q.shape[1]), row),
                pl.BlockSpec((1, tiles, N_HEADS, AUG, TM), lambda b, t: (b, 0, 0, 0, 0)),
                pl.BlockSpec((seq, WIDTH_ATT), per_b)]
    args = [q, kt, vb]
    scratch = [stat, stat, stat]
    if km is not None:
        in_specs.append(pl.BlockSpec((tiles, WIDTH_ATT), per_b))
        args.append(km)
        scratch = [pltpu.VMEM((N_HEADS, TM, AUG), BF16)] + scratch
    in_specs.append(pl.BlockSpec((TM, WIDTH_ATT), row))
    args.append(sg)
    return pl.pallas_call(
        _attn_d_prompt_kernel if km is None else _attn_c_prompt_kernel,
        out_shape=jax.ShapeDtypeStruct((n, WIDTH_ATT), BF16),
        grid=(batch, tiles),
        in_specs=in_specs,
        out_specs=pl.BlockSpec((TM, WIDTH_ATT), row),
        scratch_shapes=scratch,
        compiler_params=_cparams("arbitrary", "arbitrary"),
        name="attn_d_prompt" if km is None else "attn_c_prompt",
    )(*args)


def _page_scores(kp, qs, bias=None):
    prod = kp * qs[None]
    if bias is not None:
        prod = prod + bias
    return jnp.sum(prod, axis=-1, keepdims=True)


def _attn_c_sample_kernel(pt_ref, q_ref, kn_ref, vn_ref, sg_ref, *rest, npages):
    k_refs, v_refs = rest[:npages], rest[npages:2 * npages]
    y_ref = rest[2 * npages]
    q = q_ref[0].astype(F32)
    qs = q * SCALE
    ppb = MOBA_BLOCK // PAGE_SIZE
    nblk = npages // ppb
    gates, ms, ls, accs = [], [], [], []
    for n in range(nblk):
        kps = [k_refs[ppb * n + r][0, 0] for r in range(ppb)]
        vps = [v_refs[ppb * n + r][0, 0] for r in range(ppb)]
        ss = [_page_scores(kp, qs) for kp in kps]
        ksum = functools.reduce(lambda a, b: a + b, [jnp.sum(kp, axis=0) for kp in kps])
        gates.append(jnp.sum(q * (ksum * (1.0 / MOBA_BLOCK)), axis=-1, keepdims=True))
        m = functools.reduce(jnp.maximum, [jnp.max(s, axis=0) for s in ss])
        es = [jnp.exp(s - m[None]) for s in ss]
        ms.append(m)
        ls.append(functools.reduce(lambda a, b: a + b, [jnp.sum(e, axis=0) for e in es]))
        accs.append(functools.reduce(lambda a, b: a + b, [jnp.sum(e * vp, axis=0) for e, vp in zip(es, vps)]))
    sels = []
    for n in range(nblk):
        rank = jnp.zeros((N_HEADS, 1), F32)
        for mth in range(nblk):
            if mth == n:
                continue
            ahead = (gates[mth] >= gates[n]) if mth < n else (gates[mth] > gates[n])
            rank = rank + jnp.where(ahead, 1.0, 0.0)
        sels.append(rank < MOBA_TOPK)
    s_self = jnp.sum(qs * kn_ref[0], axis=-1, keepdims=True)
    mx = s_self
    for n in range(nblk):
        mx = jnp.maximum(mx, jnp.where(sels[n], ms[n], NEG))
    wself = jnp.exp(s_self - mx)
    l = wself
    acc = wself * vn_ref[0]
    for n in range(nblk):
        wgt = jnp.where(sels[n], jnp.exp(ms[n] - mx), 0.0)
        l = l + wgt * ls[n]
        acc = acc + wgt * accs[n]
    y_ref[0] = ((acc / l) * sg_ref[0]).astype(BF16)


def _attn_d_sample_kernel(pt_ref, q_ref, kn_ref, vn_ref, sg_ref, lft_ref, *rest, npages):
    k_refs, v_refs, lf_refs = rest[:npages], rest[npages:2 * npages], rest[2 * npages:3 * npages]
    y_ref, pad_ref = rest[3 * npages], rest[3 * npages + 1]
    b = pl.program_id(0)
    q = q_ref[0].astype(F32)
    qs = q * SCALE
    if npages * N_HEADS < LANES:
        pad_ref[...] = jnp.zeros((PAGE_SIZE, LANES), F32)
    for pg in range(npages):
        pad_ref[:, pg * N_HEADS:(pg + 1) * N_HEADS] = lf_refs[pg][0, 0]
    lft = jnp.transpose(pad_ref[...])
    r = lax.broadcasted_iota(jnp.int32, (LANES, LANES), 0)
    c = lax.broadcasted_iota(jnp.int32, (LANES, LANES), 1)
    after = jnp.where(r > c, 1.0, 0.0).astype(BF16)
    within = sum(_dot(p, after) for p in _split3(lft))
    tot = jnp.broadcast_to(within[:, 0:1] + lft[:, 0:1], (LANES, LANES))
    later = jnp.where((r % N_HEADS == c % N_HEADS) & (c // N_HEADS > r // N_HEADS), 1.0, 0.0).astype(BF16)
    carry = sum(_dot(later, p) for p in _split3(tot))
    lane = lax.broadcasted_iota(jnp.int32, (N_HEADS, lft_ref.shape[1]), 1)
    lf_new = jnp.sum(jnp.where(lane == b, lft_ref[0:N_HEADS, :], 0.0), axis=-1, keepdims=True)
    dec = within + carry
    key = lax.broadcasted_iota(jnp.int32, (PAGE_SIZE, N_HEADS, LANES), 0)
    ln3 = lax.broadcasted_iota(jnp.int32, (PAGE_SIZE, N_HEADS, LANES), 2)
    diag = key == ln3

    m = jnp.sum(qs * kn_ref[0], axis=-1, keepdims=True)
    l = jnp.ones((N_HEADS, 1), F32)
    acc = vn_ref[0]
    for pg in range(npages):
        dec_pg = dec[pg * N_HEADS:(pg + 1) * N_HEADS, :] + lf_new
        s = _page_scores(k_refs[pg][0, 0], qs, jnp.where(diag, dec_pg[None], 0.0))
        m_new = jnp.maximum(m, jnp.max(s, axis=0))
        alpha = jnp.exp(m - m_new)
        e = jnp.exp(s - m_new[None])
        l = alpha * l + jnp.sum(e, axis=0)
        acc = alpha * acc + jnp.sum(e * v_refs[pg][0, 0], axis=0)
        m = m_new
    y_ref[0] = ((acc / l) * sg_ref[0]).astype(BF16)


def _attn_sample(q, k_new, v_new, sg, lft, caches, layer, page_table):
    n = q.shape[0]
    npages = page_table.shape[1]
    r3 = lambda a: a.reshape(n, N_HEADS, HEAD_DIM)
    rowspec = pl.BlockSpec((1, N_HEADS, HEAD_DIM), lambda b, pt: (b, 0, 0))
    in_specs = [rowspec] * 4
    args = [r3(q), r3(k_new), r3(v_new), r3(sg)]
    scratch = []
    if lft is not None:
        in_specs.append(pl.BlockSpec(lft.shape, lambda b, pt: (0, 0)))
        args.append(lft)
        scratch = [pltpu.VMEM((PAGE_SIZE, LANES), F32)]
    for cache in caches:
        blk = (1, 1) + cache.shape[2:]
        tail = (0,) * (cache.ndim - 2)
        for pg in range(npages):
            in_specs.append(pl.BlockSpec(blk, lambda b, pt, pg=pg, tail=tail: (layer, pt[b * npages + pg]) + tail))
            args.append(cache)
    body = _attn_c_sample_kernel if lft is None else _attn_d_sample_kernel
    y = pl.pallas_call(
        functools.partial(body, npages=npages),
        out_shape=jax.ShapeDtypeStruct((n, N_HEADS, HEAD_DIM), BF16),
        grid_spec=pltpu.PrefetchScalarGridSpec(
            num_scalar_prefetch=1, grid=(n,), in_specs=in_specs, out_specs=rowspec, scratch_shapes=scratch),
        compiler_params=_cparams("arbitrary"),
        name="attn_c_sample" if lft is None else "attn_d_sample",
    )(page_table.reshape(-1), *args)
    return y.reshape(n, WIDTH_ATT)


def _out_pe_kernel(x_ref, y_ref, p_ref, wout_ref, peg_ref, wpg_ref, wpp_ref, xo_ref):
    x1 = x_ref[...] + _dot(y_ref[...], wout_ref[...])
    xo_ref[...] = _pe_update(x1, p_ref, peg_ref, wpg_ref, wpp_ref)


def _out_pe(x, y, p, wts, *, rows):
    n = x.shape[0]
    consts = list(wts)
    row = lambda i: (i, 0)
    return pl.pallas_call(
        _out_pe_kernel,
        out_shape=jax.ShapeDtypeStruct((n, D_MODEL), F32),
        grid=(n // rows,),
        in_specs=[pl.BlockSpec((rows, D_MODEL), row), pl.BlockSpec((rows, WIDTH_ATT), row),
                  pl.BlockSpec((rows, PLE_DIM), row)] + [_const_spec(c.shape) for c in consts],
        out_specs=pl.BlockSpec((rows, D_MODEL), row),
        compiler_params=_cparams("arbitrary"),
        name="out_pe",
    )(x, y, p, *consts)


def _rope_tables(pos):
    half = ROPE_DIM // 2
    inv = ROPE_THETA ** (-jnp.arange(half, dtype=F32) * 2.0 / ROPE_DIM)
    ang = pos.astype(F32)[:, None] * inv[None, :]
    cos, sin = jnp.cos(ang), jnp.sin(ang)
    rest = HEAD_DIM - ROPE_DIM
    ones = jnp.ones((pos.shape[0], rest), F32)
    zeros = jnp.zeros((pos.shape[0], rest), F32)
    return jnp.concatenate([cos, cos, ones], axis=-1), jnp.concatenate([-sin, sin, zeros], axis=-1)


def kernel(x_prompt, x_sample, state_conv_b, cache_k_c, cache_v_c, cache_k_d, cache_v_d, cache_logf_d, page_table, p_prompt, p_sample, norm_a, w_in_a, ln_g_a, ln_b_a, w_s_a, b_s_a, w_out_a, norm_b, w_in_b, conv_w_b, conv_b_b, ln_g_b, ln_b_b, w_out_b, norm_c, w_in_c, qn_c, kn_c, w_out_c, norm_d, w_in_d, b_f_d, qn_d, kn_d, w_out_d, pe_norm, w_pe_gate, w_pe_proj):
    bsz, seq, _ = x_prompt.shape
    nsmp = x_sample.shape[0]
    depth = pe_norm.shape[0]
    past_len = page_table.shape[1] * PAGE_SIZE
    row2 = lambda a: a.reshape(1, -1)

    xp = x_prompt.reshape(bsz * seq, D_MODEL)
    xs = x_sample.reshape(nsmp, D_MODEL)
    cos_p, sin_p = _rope_tables(jnp.arange(seq))
    cos_s, sin_s = _rope_tables(jnp.full((nsmp,), past_len))

    outs = {k: [] for k in ("va_p", "va_s", "cb_p", "cb_s", "kc_p", "vc_p", "kc_s", "vc_s",
                            "kd_p", "vd_p", "lf_p", "kd_s", "vd_s", "lf_s")}
    for i in range(depth):
        j, kind = divmod(i, 4)
        pp = p_prompt[i].reshape(bsz * seq, PLE_DIM)
        ps = p_sample[i].reshape(nsmp, PLE_DIM)
        pe = (row2(pe_norm[i]), w_pe_gate[i].astype(BF16), w_pe_proj[i].astype(BF16))
        if kind == 0:
            gw = WIDTH_A // GROUPS_A
            base = (row2(norm_a[j]), w_in_a[j].astype(BF16), row2(ln_g_a[j]), row2(ln_b_a[j]))
            tail = (w_out_a[j].astype(BF16),) + pe
            ws_p = jnp.tril(w_s_a[j]).astype(BF16)
            bs_p = jnp.repeat(jnp.transpose(b_s_a[j]), gw, axis=1)
            xp, va = _layer_a(xp, pp, base + (ws_p, bs_p) + tail, batch=bsz, seq=seq)
            ws_s = row2(jnp.repeat(w_s_a[j][:, 0, 0], gw))
            bs_s = row2(jnp.repeat(b_s_a[j][:, 0], gw))
            xs, vs = _layer_a(xs, ps, base + (ws_s, bs_s) + tail, batch=nsmp, seq=1)
            outs["va_p"].append(va)
            outs["va_s"].append(vs.reshape(nsmp, 1, WIDTH_A))
        elif kind == 1:
            wts = (row2(norm_b[j]), w_in_b[j].astype(BF16), conv_w_b[j], row2(conv_b_b[j]),
                   row2(ln_g_b[j]), row2(ln_b_b[j]), w_out_b[j].astype(BF16)) + pe
            xp, cb = _layer_b_prompt(xp, pp, wts, batch=bsz, seq=seq)
            xs, hs = _layer_b_sample(xs, ps, state_conv_b[j], wts)
            outs["cb_p"].append(cb)
            outs["cb_s"].append(jnp.concatenate([state_conv_b[j][:, 1:], hs[:, None, :]], axis=1))
        elif kind == 2:
            wts = (row2(norm_c[j]), w_in_c[j].astype(BF16), row2(qn_c[j]), row2(kn_c[j]))
            tail = (w_out_c[j].astype(BF16),) + pe
            q, k, v, sg, vb, km, kt = _proj_c(xp, wts, cos_p, sin_p, batch=bsz, seq=seq)
            y = _attn_prompt(q, kt, vb, sg, km.reshape(-1, WIDTH_ATT), batch=bsz, seq=seq)
            xp = _out_pe(xp, y, pp, tail, rows=TM)
            outs["kc_p"].append(k.reshape(bsz, seq, N_HEADS, HEAD_DIM))
            outs["vc_p"].append(v.reshape(bsz, seq, N_HEADS, HEAD_DIM))
            q, k, v, sg = _proj_c(xs, wts, cos_s, sin_s, batch=nsmp, seq=1)
            y = _attn_sample(q, k, v, sg, None, (cache_k_c, cache_v_c), j, page_table)
            xs = _out_pe(xs, y, ps, tail, rows=nsmp)
            outs["kc_s"].append(k.reshape(nsmp, 1, N_HEADS, HEAD_DIM))
            outs["vc_s"].append(v.reshape(nsmp, 1, N_HEADS, HEAD_DIM))
        else:
            wf = jnp.pad(w_in_d[j][:, 4 * WIDTH_ATT:], ((0, 0), (0, LANES - N_HEADS))).astype(BF16)
            bf = jnp.pad(b_f_d[j], (0, LANES - N_HEADS))
            wts = (row2(norm_d[j]), w_in_d[j][:, :4 * WIDTH_ATT].astype(BF16), wf, row2(bf),
                   jnp.transpose(wf), bf.reshape(LANES, 1), row2(qn_d[j]), row2(kn_d[j]))
            tail = (w_out_d[j].astype(BF16),) + pe
            q, k, v, sg, lf, vb, kt = _proj_d(xp, wts, batch=bsz, seq=seq)
            y = _attn_prompt(q, kt, vb, sg, None, batch=bsz, seq=seq)
            xp = _out_pe(xp, y, pp, tail, rows=TM)
            outs["kd_p"].append(k.reshape(bsz, seq, N_HEADS, HEAD_DIM))
            outs["vd_p"].append(v.reshape(bsz, seq, N_HEADS, HEAD_DIM))
            outs["lf_p"].append(lf.reshape(bsz, seq, N_HEADS))
            q, k, v, sg, lf, lft = _proj_d(xs, wts, batch=nsmp, seq=1)
            y = _attn_sample(q, k, v, sg, lft, (cache_k_d, cache_v_d, cache_logf_d), j, page_table)
            xs = _out_pe(xs, y, ps, tail, rows=nsmp)
            outs["kd_s"].append(k.reshape(nsmp, 1, N_HEADS, HEAD_DIM))
            outs["vd_s"].append(v.reshape(nsmp, 1, N_HEADS, HEAD_DIM))
            outs["lf_s"].append(lf.reshape(nsmp, 1, N_HEADS))

    st = lambda key: jnp.stack(outs[key])
    return (xp.reshape(bsz, seq, D_MODEL), xs.reshape(nsmp, 1, D_MODEL),
            st("va_p"), st("va_s"), st("cb_p"), st("cb_s"),
            st("kc_p"), st("vc_p"), st("kc_s"), st("vc_s"),
            st("kd_p"), st("vd_p"), st("lf_p"), st("kd_s"), st("vd_s"), st("lf_s"))
```

```python
import functools
import math

import jax
import jax.numpy as jnp
import numpy as np
from jax import lax
from jax.experimental import pallas as pl
from jax.experimental.pallas import tpu as pltpu

F32 = jnp.float32
BF16 = jnp.bfloat16

D_MODEL = 1024
PLE_DIM = 256
WIDTH_A = 2048
GROUPS_A = 8
CHUNK_A = 128
WIDTH_B = 2048
CONV_W = 31
N_HEADS = 8
HEAD_DIM = 128
WIDTH_ATT = N_HEADS * HEAD_DIM
ROPE_DIM = HEAD_DIM // 4
ROPE_THETA = 500000.0
MOBA_BLOCK = 256
MOBA_TOPK = 3
PAGE_SIZE = 128
EPS = 1e-6

LANES = 128
TM = 256
TD = 512
CONV_PAD = 32
NEG = -1e30
SCALE = HEAD_DIM ** -0.5
VMEM_LIMIT = 60 * 1024 * 1024


def _cparams(*sem):
    return pltpu.CompilerParams(dimension_semantics=sem, vmem_limit_bytes=VMEM_LIMIT)


def _const_spec(shape):
    nd = len(shape)
    return pl.BlockSpec(shape, lambda *_: (0,) * nd, pipeline_mode=pl.Buffered(1))


def _dot(a, b):
    return jnp.dot(a, b, preferred_element_type=F32)


def _dot_nt(a, b):
    return lax.dot_general(a, b, (((1,), (1,)), ((), ())), preferred_element_type=F32)


def _sigmoid(x):
    return 1.0 / (1.0 + jnp.exp(-x))


def _silu(x):
    return x * _sigmoid(x)


def _gelu(x):
    return 0.5 * x * (1.0 + jnp.tanh(math.sqrt(2.0 / math.pi) * (x + 0.044715 * (x * x * x))))


def _rms(x, g):
    return x * lax.rsqrt(jnp.mean(x * x, axis=-1, keepdims=True) + EPS) * g


def _layer_norm(x, g, b):
    xc = x - jnp.mean(x, axis=-1, keepdims=True)
    return xc * lax.rsqrt(jnp.mean(xc * xc, axis=-1, keepdims=True) + EPS) * g + b


def _log_sigmoid(x):
    return jnp.minimum(x, 0.0) - jnp.log(1.0 + jnp.exp(-jnp.abs(x)))


def _split3(x):
    hi = x.astype(BF16)
    r = x - hi.astype(F32)
    mid = r.astype(BF16)
    lo = (r - mid.astype(F32)).astype(BF16)
    return hi, mid, lo


def _pe_update(x1, p_ref, peg_ref, wpg_ref, wpp_ref):
    xn = _rms(x1, peg_ref[...]).astype(BF16)
    gate = _sigmoid(_dot(xn, wpg_ref[...]))
    return x1 + gate * _dot(p_ref[...].astype(BF16), wpp_ref[...])


def _layer_a_kernel(x_ref, p_ref, ng_ref, win_ref, lng_ref, lnb_ref, ws_ref, bs_ref, wout_ref,
                    peg_ref, wpg_ref, wpp_ref, xo_ref, vo_ref, *scratch, chunked):
    x = x_ref[...]
    rows = x.shape[0]
    xn = _rms(x, ng_ref[...]).astype(BF16)
    w = WIDTH_A
    u = _gelu(_dot(xn, win_ref[:, 0:w]))
    v = _layer_norm(_gelu(_dot(xn, win_ref[:, w:2 * w])), lng_ref[...], lnb_ref[...])
    sg = _silu(_dot(xn, win_ref[:, 2 * w:3 * w]))
    if chunked:
        (s_ref,) = scratch
        vo_ref[0] = v[rows - CHUNK_A:, :]
        vb = v.astype(BF16)
        gw = WIDTH_A // GROUPS_A
        for c in range(rows // CHUNK_A):
            rs = slice(c * CHUNK_A, (c + 1) * CHUNK_A)
            for g in range(GROUPS_A):
                cs = slice(g * gw, (g + 1) * gw)
                s_ref[rs, cs] = _dot(ws_ref[g], vb[rs, cs]) + bs_ref[:, cs]
        s = s_ref[...]
    else:
        vo_ref[...] = v
        s = v * ws_ref[...] + bs_ref[...]
    y = (u * s * sg).astype(BF16)
    x1 = x + _dot(y, wout_ref[...])
    xo_ref[...] = _pe_update(x1, p_ref, peg_ref, wpg_ref, wpp_ref)


def _p_spec(rows, row, layer):
    return pl.BlockSpec((None, rows, PLE_DIM), lambda *g: (layer,) + row(*g))


def _layer_a(x, p, layer, wts, *, batch, seq):
    n = x.shape[0]
    chunked = seq > 1
    ng, win, lng, lnb, ws, bs, wout, peg, wpg, wpp = wts
    if chunked:
        tiles = seq // TD
        grid = (batch, tiles)
        row = lambda b, t: (b * tiles + t, 0)
        rows = TD
        vo_shape = jax.ShapeDtypeStruct((batch, CHUNK_A, WIDTH_A), F32)
        vo_spec = pl.BlockSpec((1, CHUNK_A, WIDTH_A), lambda b, t: (b, 0, 0))
        scratch = [pltpu.VMEM((TD, WIDTH_A), F32)]
    else:
        grid = (1, 1)
        row = lambda b, t: (0, 0)
        rows = n
        vo_shape = jax.ShapeDtypeStruct((n, WIDTH_A), F32)
        vo_spec = pl.BlockSpec((n, WIDTH_A), row)
        scratch = []
    consts = [ng, win, lng, lnb, ws, bs, wout, peg, wpg, wpp]
    return pl.pallas_call(
        functools.partial(_layer_a_kernel, chunked=chunked),
        out_shape=(jax.ShapeDtypeStruct((n, D_MODEL), F32), vo_shape),
        grid=grid,
        in_specs=[pl.BlockSpec((rows, D_MODEL), row), _p_spec(rows, row, layer)]
        + [_const_spec(c.shape) for c in consts],
        out_specs=(pl.BlockSpec((rows, D_MODEL), row), vo_spec),
        scratch_shapes=scratch,
        compiler_params=_cparams("arbitrary", "arbitrary"),
        name="layer_a_prompt" if chunked else "layer_a_sample",
    )(x, p, *consts)


CONV_ROWS = 64


def _layer_b_prompt_kernel(x_ref, p_ref, ng_ref, wab_ref, wg_ref, cw_ref, cb_ref, lng_ref, lnb_ref, wout_ref,
                           peg_ref, wpg_ref, wpp_ref, xo_ref, st_ref, hext_ref, c_ref):
    t = pl.program_id(1)
    ncb = WIDTH_B // LANES
    npair = wab_ref.shape[0] // 2
    rows = x_ref.shape[0]

    @pl.when(t == 0)
    def _():
        hext_ref[:, 0:CONV_PAD, :] = jnp.zeros((ncb, CONV_PAD, LANES), F32)

    x = x_ref[...]
    xn = _rms(x, ng_ref[...]).astype(BF16)

    def glu_cols(c):
        h2 = _dot(xn, wab_ref[c]) * _sigmoid(_dot(xn, wab_ref[npair + c]))
        hext_ref[2 * c, CONV_PAD:CONV_PAD + rows, :] = h2[:, 0:LANES]
        hext_ref[2 * c + 1, CONV_PAD:CONV_PAD + rows, :] = h2[:, LANES:2 * LANES]

    base = CONV_PAD - (CONV_W - 1)

    def conv_cols(cb):
        for rg in range(rows // CONV_ROWS):
            accs = [None] * (CONV_ROWS // 8)
            for k in range(CONV_W):
                wk = cw_ref[cb, k:k + 1, :]
                for j in range(CONV_ROWS // 8):
                    r0 = rg * CONV_ROWS + j * 8 + base + k
                    term = wk * hext_ref[cb, r0:r0 + 8, :]
                    accs[j] = term if accs[j] is None else accs[j] + term
            for j in range(CONV_ROWS // 8):
                r0 = rg * CONV_ROWS + j * 8
                c_ref[cb, r0:r0 + 8, :] = accs[j]

    glu_cols(0)

    def body(c, carry):
        conv_cols(2 * c)
        conv_cols(2 * c + 1)
        glu_cols(c + 1)
        return carry

    lax.fori_loop(0, npair - 1, body, 0)
    conv_cols(ncb - 2)
    conv_cols(ncb - 1)

    sg = _silu(_dot(xn, wg_ref[...]))
    for cb in range(ncb):
        st_ref[0, :, cb * LANES:(cb + 1) * LANES] = hext_ref[cb, rows + base:rows + CONV_PAD, :]
        hext_ref[cb, 0:CONV_PAD, :] = hext_ref[cb, rows:rows + CONV_PAD, :]
    c = jnp.concatenate([c_ref[cb] for cb in range(ncb)], axis=-1) + cb_ref[...]
    c = _silu(_layer_norm(c, lng_ref[...], lnb_ref[...]))
    y = (c * sg).astype(BF16)
    x1 = x + _dot(y, wout_ref[...])
    xo_ref[...] = _pe_update(x1, p_ref, peg_ref, wpg_ref, wpp_ref)


def _layer_b_prompt(x, p, layer, wts, *, batch, seq):
    n = x.shape[0]
    tiles = seq // TD
    ng, win, cw, cb, lng, lnb, wout, peg, wpg, wpp = wts
    ncb = WIDTH_B // LANES
    cw3 = jnp.transpose(cw.reshape(CONV_W, ncb, LANES), (1, 0, 2))
    wab = jnp.transpose(win[:, 0:2 * WIDTH_B].reshape(D_MODEL, ncb, 2 * LANES), (1, 0, 2))
    consts = [ng, wab, win[:, 2 * WIDTH_B:], cw3, cb, lng, lnb, wout, peg, wpg, wpp]
    row = lambda b, t: (b * tiles + t, 0)
    return pl.pallas_call(
        _layer_b_prompt_kernel,
        out_shape=(jax.ShapeDtypeStruct((n, D_MODEL), F32),
                   jax.ShapeDtypeStruct((batch, CONV_W - 1, WIDTH_B), F32)),
        grid=(batch, tiles),
        in_specs=[pl.BlockSpec((TD, D_MODEL), row), _p_spec(TD, row, layer)]
        + [_const_spec(c.shape) for c in consts],
        out_specs=(pl.BlockSpec((TD, D_MODEL), row),
                   pl.BlockSpec((1, CONV_W - 1, WIDTH_B), lambda b, t: (b, 0, 0))),
        scratch_shapes=[pltpu.VMEM((ncb, TD + CONV_PAD, LANES), F32), pltpu.VMEM((ncb, TD, LANES), F32)],
        compiler_params=_cparams("arbitrary", "arbitrary"),
        name="layer_b_prompt",
    )(x, p, *consts)


SAMPLE_CONV_ROWS = 16


def _layer_b_sample_kernel(x_ref, p_ref, st_ref, ng_ref, win_ref, cw_ref, cb_ref, lng_ref, lnb_ref, wout_ref,
                           peg_ref, wpg_ref, wpp_ref, xo_ref, ho_ref, h_ref, sg_ref, c_ref):
    i = pl.program_id(0)
    w = WIDTH_B

    @pl.when(i == 0)
    def _():
        xn = _rms(x_ref[...], ng_ref[...]).astype(BF16)
        h = _dot(xn, win_ref[:, 0:w]) * _sigmoid(_dot(xn, win_ref[:, w:2 * w]))
        h_ref[...] = h
        ho_ref[...] = h
        sg_ref[...] = _silu(_dot(xn, win_ref[:, 2 * w:3 * w]))

    r0 = pl.multiple_of(i * SAMPLE_CONV_ROWS, SAMPLE_CONV_ROWS)
    acc = cb_ref[...] + cw_ref[CONV_W - 1:CONV_W, :] * h_ref[pl.ds(r0, SAMPLE_CONV_ROWS), :]
    for k in range(CONV_W - 1):
        acc = acc + cw_ref[k:k + 1, :] * st_ref[k]
    c_ref[pl.ds(r0, SAMPLE_CONV_ROWS), :] = acc

    @pl.when(i == pl.num_programs(0) - 1)
    def _():
        c = _silu(_layer_norm(c_ref[...], lng_ref[...], lnb_ref[...]))
        y = (c * sg_ref[...]).astype(BF16)
        x1 = x_ref[...] + _dot(y, wout_ref[...])
        xo_ref[...] = _pe_update(x1, p_ref, peg_ref, wpg_ref, wpp_ref)


def _layer_b_sample(x, p, layer, state, wts):
    n = x.shape[0]
    consts = list(wts)
    full = lambda i: (0, 0)
    return pl.pallas_call(
        _layer_b_sample_kernel,
        out_shape=(jax.ShapeDtypeStruct((n, D_MODEL), F32), jax.ShapeDtypeStruct((n, WIDTH_B), F32)),
        grid=(n // SAMPLE_CONV_ROWS,),
        in_specs=[pl.BlockSpec((n, D_MODEL), full), _p_spec(n, full, layer),
                  pl.BlockSpec((CONV_W - 1, SAMPLE_CONV_ROWS, WIDTH_B), lambda i: (0, i, 0))]
        + [_const_spec(c.shape) for c in consts],
        out_specs=(pl.BlockSpec((n, D_MODEL), full), pl.BlockSpec((n, WIDTH_B), full)),
        scratch_shapes=[pltpu.VMEM((n, WIDTH_B), F32)] * 3,
        compiler_params=_cparams("arbitrary"),
        name="layer_b_sample",
    )(x, p, state, *consts)


AUG = 2 * HEAD_DIM


def _head_rms(xh, g):
    return xh * lax.rsqrt(jnp.mean(xh * xh, axis=-1, keepdims=True) + EPS) * g


def _rope(xh, cos, sin):
    half = ROPE_DIM // 2
    lane = lax.broadcasted_iota(jnp.int32, xh.shape, 1)
    partner = jnp.where(lane < half, pltpu.roll(xh, LANES - half, 1), pltpu.roll(xh, half, 1))
    return xh * cos + partner * sin


def _proj_c_kernel(x_ref, ng_ref, win_ref, qn_ref, kn_ref, cos_ref, sin_ref,
                   q_ref, k_ref, v_ref, sg_ref, *rest, prompt):
    xn = _rms(x_ref[...], ng_ref[...]).astype(BF16)
    rows = xn.shape[0]
    w = WIDTH_ATT
    cos = cos_ref[...]
    sin = sin_ref[...]
    qf = _dot(xn, win_ref[:, 0:w])
    kf = _dot(xn, win_ref[:, w:2 * w])
    if prompt:
        vb_ref, km_ref, kt_ref = rest
        sub = lax.broadcasted_iota(jnp.int32, (HEAD_DIM, rows), 0)
        onehot = jnp.where(sub == pl.program_id(1), 1.0, 0.0).astype(BF16)
    for h in range(N_HEADS):
        hs = slice(h * HEAD_DIM, (h + 1) * HEAD_DIM)
        q_ref[:, hs] = _rope(_head_rms(qf[:, hs], qn_ref[...]), cos, sin).astype(BF16)
        kh = _rope(_head_rms(kf[:, hs], kn_ref[...]), cos, sin)
        k_ref[:, hs] = kh
        if prompt:
            km_ref[0, :, hs] = jnp.mean(kh, axis=0, keepdims=True)
            kt_ref[0, 0, h, 0:HEAD_DIM, :] = jnp.transpose(kh).astype(BF16)
            kt_ref[0, 0, h, HEAD_DIM:AUG, :] = onehot
    v = _dot(xn, win_ref[:, 2 * w:3 * w])
    v_ref[...] = v
    if prompt:
        vb_ref[...] = v.astype(BF16)
    sg_ref[...] = _silu(_dot(xn, win_ref[:, 3 * w:4 * w]))


def _proj_c(x, wts, cos, sin, *, batch, seq):
    n = x.shape[0]
    prompt = seq > 1
    rows = TM if prompt else n
    tiles = seq // rows if prompt else 1
    nb = batch if prompt else 1
    consts = list(wts)
    row = lambda b, t: (b * tiles + t, 0)
    pos = lambda b, t: (t, 0)
    wide = jax.ShapeDtypeStruct((n, WIDTH_ATT), F32)
    wide_b = jax.ShapeDtypeStruct((n, WIDTH_ATT), BF16)
    spec = pl.BlockSpec((rows, WIDTH_ATT), row)
    out_shape = [wide_b, wide, wide, wide]
    out_specs = [spec] * 4
    if prompt:
        out_shape += [wide_b, jax.ShapeDtypeStruct((nb * tiles, 1, WIDTH_ATT), F32),
                      jax.ShapeDtypeStruct((nb, tiles, N_HEADS, AUG, rows), BF16)]
        out_specs += [spec, pl.BlockSpec((1, 1, WIDTH_ATT), lambda b, t: (b * tiles + t, 0, 0)),
                      pl.BlockSpec((1, 1, N_HEADS, AUG, rows), lambda b, t: (b, t, 0, 0, 0))]
    return pl.pallas_call(
        functools.partial(_proj_c_kernel, prompt=prompt),
        out_shape=tuple(out_shape),
        grid=(nb, tiles),
        in_specs=[pl.BlockSpec((rows, D_MODEL), row)] + [_const_spec(c.shape) for c in consts]
        + [pl.BlockSpec((rows, HEAD_DIM), pos), pl.BlockSpec((rows, HEAD_DIM), pos)],
        out_specs=tuple(out_specs),
        compiler_params=_cparams("arbitrary", "arbitrary"),
        name="proj_c_prompt" if prompt else "proj_c_sample",
    )(x, *consts, cos, sin)


def _proj_d_kernel(x_ref, ng_ref, win_ref, wf_ref, bf_ref, wft_ref, bft_ref, qn_ref, kn_ref,
                   q_ref, k_ref, v_ref, sg_ref, lf_ref, *rest, prompt):
    xn = _rms(x_ref[...], ng_ref[...]).astype(BF16)
    rows = xn.shape[0]
    w = WIDTH_ATT
    qf = _dot(xn, win_ref[:, 0:w])
    kf = _dot(xn, win_ref[:, w:2 * w])
    v = _dot(xn, win_ref[:, 2 * w:3 * w])
    v_ref[...] = v
    sg_ref[...] = _silu(_dot(xn, win_ref[:, 3 * w:4 * w]))
    lane = lax.broadcasted_iota(jnp.int32, (rows, LANES), 1)
    logf = jnp.where(lane < N_HEADS, _log_sigmoid(_dot(xn, wf_ref[...]) + bf_ref[...]), 0.0)
    lf_ref[...] = logf[:, 0:N_HEADS]
    if not prompt:
        (lft_ref,) = rest
        for h in range(N_HEADS):
            hs = slice(h * HEAD_DIM, (h + 1) * HEAD_DIM)
            q_ref[:, hs] = _head_rms(qf[:, hs], qn_ref[...]).astype(BF16)
            k_ref[:, hs] = _head_rms(kf[:, hs], kn_ref[...])
        lft_ref[...] = _log_sigmoid(_dot_nt(wft_ref[...], xn) + bft_ref[...])
        return

    vb_ref, kt_ref, carry_ref = rest
    vb_ref[...] = v.astype(BF16)

    @pl.when(pl.program_id(1) == 0)
    def _():
        carry_ref[...] = jnp.zeros((1, LANES), F32)

    r = lax.broadcasted_iota(jnp.int32, (rows, rows), 0)
    c = lax.broadcasted_iota(jnp.int32, (rows, rows), 1)
    tri = jnp.where(c <= r, 1.0, 0.0).astype(BF16)
    hi, mid, lo = _split3(logf)
    cum = _dot(tri, hi) + _dot(tri, mid) + _dot(tri, lo) + carry_ref[...]
    carry_ref[...] = cum[rows - 1:rows, :]
    pieces = [p.astype(F32) for p in _split3(cum * (1.0 / SCALE))]
    pieces_t = [jnp.transpose(p) for p in pieces]
    sub = lax.broadcasted_iota(jnp.int32, (HEAD_DIM, rows), 0)
    for h in range(N_HEADS):
        hs = slice(h * HEAD_DIM, (h + 1) * HEAD_DIM)
        c1, c2, c3 = [p[:, h:h + 1] for p in pieces]
        aug_q = jnp.where(lane == 0, c1, jnp.where(lane == 1, c2, jnp.where(lane == 2, c3,
                          jnp.where(lane < 6, 1.0, 0.0))))
        q_ref[:, h * AUG:h * AUG + HEAD_DIM] = _head_rms(qf[:, hs], qn_ref[...]).astype(BF16)
        q_ref[:, h * AUG + HEAD_DIM:(h + 1) * AUG] = aug_q.astype(BF16)
        kh = _head_rms(kf[:, hs], kn_ref[...])
        k_ref[:, hs] = kh
        d1, d2, d3 = [p[h:h + 1, :] for p in pieces_t]
        aug_k = jnp.where(sub == 3, -d1, jnp.where(sub == 4, -d2, jnp.where(sub == 5, -d3,
                          jnp.where(sub < 3, 1.0, 0.0))))
        kt_ref[0, 0, h, 0:HEAD_DIM, :] = jnp.transpose(kh).astype(BF16)
        kt_ref[0, 0, h, HEAD_DIM:AUG, :] = aug_k.astype(BF16)


def _proj_d(x, wts, *, batch, seq):
    n = x.shape[0]
    prompt = seq > 1
    rows = TM if prompt else n
    tiles = seq // rows if prompt else 1
    nb = batch if prompt else 1
    consts = list(wts)
    row = lambda b, t: (b * tiles + t, 0)
    wide = jax.ShapeDtypeStruct((n, WIDTH_ATT), F32)
    spec = pl.BlockSpec((rows, WIDTH_ATT), row)
    qw = N_HEADS * AUG if prompt else WIDTH_ATT
    out_shape = [jax.ShapeDtypeStruct((n, qw), BF16), wide, wide, wide, jax.ShapeDtypeStruct((n, N_HEADS), F32)]
    out_specs = [pl.BlockSpec((rows, qw), row), spec, spec, spec, pl.BlockSpec((rows, N_HEADS), row)]
    scratch = []
    if prompt:
        out_shape += [jax.ShapeDtypeStruct((n, WIDTH_ATT), BF16),
                      jax.ShapeDtypeStruct((nb, tiles, N_HEADS, AUG, rows), BF16)]
        out_specs += [spec, pl.BlockSpec((1, 1, N_HEADS, AUG, rows), lambda b, t: (b, t, 0, 0, 0))]
        scratch = [pltpu.VMEM((1, LANES), F32)]
    else:
        out_shape.append(jax.ShapeDtypeStruct((LANES, n), F32))
        out_specs.append(pl.BlockSpec((LANES, n), lambda b, t: (0, 0)))
    return pl.pallas_call(
        functools.partial(_proj_d_kernel, prompt=prompt),
        out_shape=tuple(out_shape),
        grid=(nb, tiles),
        in_specs=[pl.BlockSpec((rows, D_MODEL), row)] + [_const_spec(c.shape) for c in consts],
        out_specs=tuple(out_specs),
        scratch_shapes=scratch,
        compiler_params=_cparams("arbitrary", "arbitrary"),
        name="proj_d_prompt" if prompt else "proj_d_sample",
    )(x, *consts)


EXP2_SCALE = SCALE * math.log2(math.e)


def _flash_block(j, qa, kt_ref, vb_ref, m_ref, l_ref, acc_ref, causal):
    blk = TM
    r0 = pl.multiple_of(j * blk, blk)
    for h in range(N_HEADS):
        hs = slice(h * HEAD_DIM, (h + 1) * HEAD_DIM)
        s = _dot(qa(h), kt_ref[0, j, h])
        if causal is not None:
            s = jnp.where(causal, s, NEG)
        s0, s1 = s[:, 0:LANES], s[:, LANES:2 * LANES]
        m_old = m_ref[h]
        m_new = jnp.maximum(m_old, jnp.max(jnp.maximum(s0, s1), axis=-1, keepdims=True))
        alpha = jnp.exp2((m_old - m_new) * EXP2_SCALE)
        p0 = jnp.exp2((s0 - m_new) * EXP2_SCALE)
        p1 = jnp.exp2((s1 - m_new) * EXP2_SCALE)
        l_ref[h] = alpha * l_ref[h] + jnp.sum(p0 + p1, axis=-1, keepdims=True)
        pv = _dot(jnp.concatenate([p0, p1], axis=-1).astype(BF16), vb_ref[pl.ds(r0, blk), hs])
        acc_ref[h] = alpha * acc_ref[h] + pv
        m_ref[h] = m_new


def _flash_all(i, qa, kt_ref, vb_ref, sg_ref, y_ref, m_ref, l_ref, acc_ref):
    blk = TM
    m_ref[...] = jnp.full(m_ref.shape, NEG, F32)
    l_ref[...] = jnp.zeros(l_ref.shape, F32)
    acc_ref[...] = jnp.zeros(acc_ref.shape, F32)

    def body(j, carry):
        _flash_block(j, qa, kt_ref, vb_ref, m_ref, l_ref, acc_ref, None)
        return carry

    lax.fori_loop(0, i, body, 0)
    row = lax.broadcasted_iota(jnp.int32, (blk, blk), 0)
    col = lax.broadcasted_iota(jnp.int32, (blk, blk), 1)
    _flash_block(i, qa, kt_ref, vb_ref, m_ref, l_ref, acc_ref, col <= row)
    for h in range(N_HEADS):
        hs = slice(h * HEAD_DIM, (h + 1) * HEAD_DIM)
        y_ref[:, hs] = ((acc_ref[h] / l_ref[h]) * sg_ref[:, hs]).astype(BF16)


def _attn_c_prompt_kernel(q_ref, kt_ref, vb_ref, km_ref, sg_ref, y_ref, qa_ref, m_ref, l_ref, acc_ref):
    i = pl.program_id(1)
    blk = MOBA_BLOCK
    nblk = km_ref.shape[0]
    sub = lax.broadcasted_iota(jnp.int32, (nblk, blk), 0)
    valid = sub < i
    for h in range(N_HEADS):
        hs = slice(h * HEAD_DIM, (h + 1) * HEAD_DIM)
        qh = q_ref[:, hs]
        km = km_ref[:, hs]
        km_hi = km.astype(BF16)
        km_lo = (km - km_hi.astype(F32)).astype(BF16)
        gate = _dot_nt(km_hi, qh) + _dot_nt(km_lo, qh)
        bias = jnp.zeros((nblk, blk), F32)
        for n in range(nblk):
            gn = gate[n:n + 1, :]
            ahead = valid & ((gate > gn) | ((gate == gn) & (sub < n)))
            rank = jnp.sum(jnp.where(ahead, 1.0, 0.0), axis=0, keepdims=True)
            bias = jnp.where((sub == n) & (rank >= MOBA_TOPK) & valid, NEG, bias)
        bias = jnp.concatenate([bias, jnp.zeros((LANES - nblk, blk), F32)], axis=0)
        qa_ref[h, :, 0:HEAD_DIM] = qh
        qa_ref[h, :, HEAD_DIM:AUG] = jnp.transpose(bias).astype(BF16)
    _flash_all(i, lambda h: qa_ref[h], kt_ref, vb_ref, sg_ref, y_ref, m_ref, l_ref, acc_ref)


def _attn_d_prompt_kernel(q_ref, kt_ref, vb_ref, sg_ref, y_ref, m_ref, l_ref, acc_ref):
    i = pl.program_id(1)
    _flash_all(i, lambda h: q_ref[:, h * AUG:(h + 1) * AUG], kt_ref, vb_ref, sg_ref, y_ref, m_ref, l_ref, acc_ref)


def _attn_prompt(q, kt, vb, sg, km, *, batch, seq):
    n = q.shape[0]
    tiles = seq // TM
    row = lambda b, t: (b * tiles + t, 0)
    per_b = lambda b, t: (b, 0)
    stat = pltpu.VMEM((N_HEADS, TM, LANES), F32)
    in_specs = [pl.BlockSpec((TM, q.shape[1]), row),
                pl.BlockSpec((1, tiles, N_HEADS, AUG, TM), lambda b, t: (b, 0, 0, 0, 0)),
                pl.BlockSpec((seq, WIDTH_ATT), per_b)]
    args = [q, kt, vb]
    scratch = [stat, stat, stat]
    if km is not None:
        in_specs.append(pl.BlockSpec((tiles, WIDTH_ATT), per_b))
        args.append(km)
        scratch = [pltpu.VMEM((N_HEADS, TM, AUG), BF16)] + scratch
    in_specs.append(pl.BlockSpec((TM, WIDTH_ATT), row))
    args.append(sg)
    return pl.pallas_call(
        _attn_d_prompt_kernel if km is None else _attn_c_prompt_kernel,
        out_shape=jax.ShapeDtypeStruct((n, WIDTH_ATT), BF16),
        grid=(batch, tiles),
        in_specs=in_specs,
        out_specs=pl.BlockSpec((TM, WIDTH_ATT), row),
        scratch_shapes=scratch,
        compiler_params=_cparams("arbitrary", "arbitrary"),
        name="attn_d_prompt" if km is None else "attn_c_prompt",
    )(*args)


LOG2E = math.log2(math.e)


def _page_scores(kp, qs, bias=None):
    prod = kp * qs[None]
    if bias is not None:
        prod = prod + bias
    return jnp.sum(prod, axis=-1, keepdims=True)


def _attn_c_sample_kernel(pt_ref, q_ref, kn_ref, vn_ref, sg_ref, *rest, npages):
    k_refs, v_refs = rest[:npages], rest[npages:2 * npages]
    y_ref = rest[2 * npages]
    q = q_ref[0].astype(F32)
    qs = q * (SCALE * LOG2E)
    ppb = MOBA_BLOCK // PAGE_SIZE
    nblk = npages // ppb
    gates, ms, ls, accs = [], [], [], []
    for n in range(nblk):
        kps = [k_refs[ppb * n + r][0, 0] for r in range(ppb)]
        vps = [v_refs[ppb * n + r][0, 0] for r in range(ppb)]
        ss = [_page_scores(kp, qs) for kp in kps]
        ksum = functools.reduce(lambda a, b: a + b, [jnp.sum(kp, axis=0) for kp in kps])
        gates.append(jnp.sum(q * (ksum * (1.0 / MOBA_BLOCK)), axis=-1, keepdims=True))
        m = functools.reduce(jnp.maximum, [jnp.max(s, axis=0) for s in ss])
        es = [jnp.exp2(s - m[None]) for s in ss]
        ms.append(m)
        ls.append(functools.reduce(lambda a, b: a + b, [jnp.sum(e, axis=0) for e in es]))
        accs.append(functools.reduce(lambda a, b: a + b, [jnp.sum(e * vp, axis=0) for e, vp in zip(es, vps)]))
    sels = []
    for n in range(nblk):
        rank = jnp.zeros((N_HEADS, 1), F32)
        for mth in range(nblk):
            if mth == n:
                continue
            ahead = (gates[mth] >= gates[n]) if mth < n else (gates[mth] > gates[n])
            rank = rank + jnp.where(ahead, 1.0, 0.0)
        sels.append(rank < MOBA_TOPK)
    s_self = jnp.sum(qs * kn_ref[0], axis=-1, keepdims=True)
    mx = s_self
    for n in range(nblk):
        mx = jnp.maximum(mx, jnp.where(sels[n], ms[n], NEG))
    wself = jnp.exp2(s_self - mx)
    l = wself
    acc = wself * vn_ref[0]
    for n in range(nblk):
        wgt = jnp.where(sels[n], jnp.exp2(ms[n] - mx), 0.0)
        l = l + wgt * ls[n]
        acc = acc + wgt * accs[n]
    y_ref[0] = ((acc / l) * sg_ref[0]).astype(BF16)


def _attn_d_sample_kernel(pt_ref, q_ref, kn_ref, vn_ref, sg_ref, lft_ref, *rest, npages):
    k_refs, v_refs, lf_refs = rest[:npages], rest[npages:2 * npages], rest[2 * npages:3 * npages]
    y_ref = rest[3 * npages]
    b = pl.program_id(0)
    q = q_ref[0].astype(F32)
    qs = q * (SCALE * LOG2E)
    blocks = [lf_refs[pg][0, 0] for pg in range(npages)]
    if npages * N_HEADS < LANES:
        blocks.append(jnp.zeros((LANES - npages * N_HEADS, PAGE_SIZE), F32))
    lft = jnp.concatenate(blocks, axis=0)
    r = lax.broadcasted_iota(jnp.int32, (LANES, LANES), 0)
    c = lax.broadcasted_iota(jnp.int32, (LANES, LANES), 1)
    after = jnp.where(r > c, 1.0, 0.0).astype(BF16)
    within = sum(_dot(p, after) for p in _split3(lft))
    tot = jnp.broadcast_to(within[:, 0:1] + lft[:, 0:1], (LANES, LANES))
    later = jnp.where((r % N_HEADS == c % N_HEADS) & (c // N_HEADS > r // N_HEADS), 1.0, 0.0).astype(BF16)
    carry = sum(_dot(later, p) for p in _split3(tot))
    lane = lax.broadcasted_iota(jnp.int32, (N_HEADS, lft_ref.shape[1]), 1)
    lf_new = jnp.sum(jnp.where(lane == b, lft_ref[0:N_HEADS, :], 0.0), axis=-1, keepdims=True)
    dec = (within + carry) * LOG2E
    key = lax.broadcasted_iota(jnp.int32, (PAGE_SIZE, N_HEADS, LANES), 0)
    ln3 = lax.broadcasted_iota(jnp.int32, (PAGE_SIZE, N_HEADS, LANES), 2)
    diag = jnp.where(key == ln3, 1.0, 0.0)

    ms, ls, accs = [], [], []
    for pg in range(npages):
        dec_pg = dec[pg * N_HEADS:(pg + 1) * N_HEADS, :] + lf_new * LOG2E
        s = _page_scores(k_refs[pg][0, 0], qs, diag * dec_pg[None])
        m = jnp.max(s, axis=0)
        e = jnp.exp2(s - m[None])
        ms.append(m)
        ls.append(jnp.sum(e, axis=0))
        accs.append(jnp.sum(e * v_refs[pg][0, 0], axis=0))
    s_self = jnp.sum(qs * kn_ref[0], axis=-1, keepdims=True)
    mx = functools.reduce(jnp.maximum, ms, s_self)
    wself = jnp.exp2(s_self - mx)
    l = wself
    acc = wself * vn_ref[0]
    for pg in range(npages):
        wgt = jnp.exp2(ms[pg] - mx)
        l = l + wgt * ls[pg]
        acc = acc + wgt * accs[pg]
    y_ref[0] = ((acc / l) * sg_ref[0]).astype(BF16)


def _attn_sample(q, k_new, v_new, sg, lft, caches, layer, page_table):
    n = q.shape[0]
    npages = page_table.shape[1]
    r3 = lambda a: a.reshape(n, N_HEADS, HEAD_DIM)
    rowspec = pl.BlockSpec((1, N_HEADS, HEAD_DIM), lambda b, pt: (b, 0, 0))
    in_specs = [rowspec] * 4
    args = [r3(q), r3(k_new), r3(v_new), r3(sg)]
    if lft is not None:
        in_specs.append(pl.BlockSpec(lft.shape, lambda b, pt: (0, 0)))
        args.append(lft)
    for cache in caches:
        blk = (1, 1) + cache.shape[2:]
        tail = (0,) * (cache.ndim - 2)
        for pg in range(npages):
            in_specs.append(pl.BlockSpec(blk, lambda b, pt, pg=pg, tail=tail: (layer, pt[b * npages + pg]) + tail))
            args.append(cache)
    body = _attn_c_sample_kernel if lft is None else _attn_d_sample_kernel
    y = pl.pallas_call(
        functools.partial(body, npages=npages),
        out_shape=jax.ShapeDtypeStruct((n, N_HEADS, HEAD_DIM), BF16),
        grid_spec=pltpu.PrefetchScalarGridSpec(
            num_scalar_prefetch=1, grid=(n,), in_specs=in_specs, out_specs=rowspec),
        compiler_params=_cparams("arbitrary"),
        name="attn_c_sample" if lft is None else "attn_d_sample",
    )(page_table.reshape(-1), *args)
    return y.reshape(n, WIDTH_ATT)


def _out_pe_kernel(x_ref, y_ref, p_ref, wout_ref, peg_ref, wpg_ref, wpp_ref, xo_ref):
    x1 = x_ref[...] + _dot(y_ref[...], wout_ref[...])
    xo_ref[...] = _pe_update(x1, p_ref, peg_ref, wpg_ref, wpp_ref)


def _out_pe(x, y, p, layer, wts, *, rows):
    n = x.shape[0]
    consts = list(wts)
    row = lambda i: (i, 0)
    return pl.pallas_call(
        _out_pe_kernel,
        out_shape=jax.ShapeDtypeStruct((n, D_MODEL), F32),
        grid=(n // rows,),
        in_specs=[pl.BlockSpec((rows, D_MODEL), row), pl.BlockSpec((rows, WIDTH_ATT), row),
                  _p_spec(rows, row, layer)] + [_const_spec(c.shape) for c in consts],
        out_specs=pl.BlockSpec((rows, D_MODEL), row),
        compiler_params=_cparams("arbitrary"),
        name="out_pe",
    )(x, y, p, *consts)


def _rope_tables(pos):
    half = ROPE_DIM // 2
    inv = ROPE_THETA ** (-jnp.arange(half, dtype=F32) * 2.0 / ROPE_DIM)
    ang = pos.astype(F32)[:, None] * inv[None, :]
    cos, sin = jnp.cos(ang), jnp.sin(ang)
    rest = HEAD_DIM - ROPE_DIM
    ones = jnp.ones((pos.shape[0], rest), F32)
    zeros = jnp.zeros((pos.shape[0], rest), F32)
    return jnp.concatenate([cos, cos, ones], axis=-1), jnp.concatenate([-sin, sin, zeros], axis=-1)


def kernel(x_prompt, x_sample, state_conv_b, cache_k_c, cache_v_c, cache_k_d, cache_v_d, cache_logf_d, page_table, p_prompt, p_sample, norm_a, w_in_a, ln_g_a, ln_b_a, w_s_a, b_s_a, w_out_a, norm_b, w_in_b, conv_w_b, conv_b_b, ln_g_b, ln_b_b, w_out_b, norm_c, w_in_c, qn_c, kn_c, w_out_c, norm_d, w_in_d, b_f_d, qn_d, kn_d, w_out_d, pe_norm, w_pe_gate, w_pe_proj):
    bsz, seq, _ = x_prompt.shape
    nsmp = x_sample.shape[0]
    depth = pe_norm.shape[0]
    past_len = page_table.shape[1] * PAGE_SIZE
    row2 = lambda a: a.reshape(1, -1)

    pp = p_prompt.reshape(depth, bsz * seq, PLE_DIM)
    ps = p_sample.reshape(depth, nsmp, PLE_DIM)
    xp = x_prompt.reshape(bsz * seq, D_MODEL)
    xs = x_sample.reshape(nsmp, D_MODEL)
    cos_p, sin_p = _rope_tables(jnp.arange(seq))
    cos_s, sin_s = _rope_tables(jnp.full((nsmp,), past_len))

    outs = {k: [] for k in ("va_p", "va_s", "cb_p", "cb_s", "kc_p", "vc_p", "kc_s", "vc_s",
                            "kd_p", "vd_p", "lf_p", "kd_s", "vd_s", "lf_s")}
    for i in range(depth):
        j, kind = divmod(i, 4)
        pe = (row2(pe_norm[i]), w_pe_gate[i].astype(BF16), w_pe_proj[i].astype(BF16))
        if kind == 0:
            gw = WIDTH_A // GROUPS_A
            base = (row2(norm_a[j]), w_in_a[j].astype(BF16), row2(ln_g_a[j]), row2(ln_b_a[j]))
            tail = (w_out_a[j].astype(BF16),) + pe
            ws_p = jnp.tril(w_s_a[j]).astype(BF16)
            bs_p = jnp.repeat(jnp.transpose(b_s_a[j]), gw, axis=1)
            xp, va = _layer_a(xp, pp, i, base + (ws_p, bs_p) + tail, batch=bsz, seq=seq)
            ws_s = row2(jnp.repeat(w_s_a[j][:, 0, 0], gw))
            bs_s = row2(jnp.repeat(b_s_a[j][:, 0], gw))
            xs, vs = _layer_a(xs, ps, i, base + (ws_s, bs_s) + tail, batch=nsmp, seq=1)
            outs["va_p"].append(va)
            outs["va_s"].append(vs.reshape(nsmp, 1, WIDTH_A))
        elif kind == 1:
            wts = (row2(norm_b[j]), w_in_b[j].astype(BF16), conv_w_b[j], row2(conv_b_b[j]),
                   row2(ln_g_b[j]), row2(ln_b_b[j]), w_out_b[j].astype(BF16)) + pe
            xp, cb = _layer_b_prompt(xp, pp, i, wts, batch=bsz, seq=seq)
            xs, hs = _layer_b_sample(xs, ps, i, jnp.transpose(state_conv_b[j], (1, 0, 2)), wts)
            outs["cb_p"].append(cb)
            outs["cb_s"].append(jnp.concatenate([state_conv_b[j][:, 1:], hs[:, None, :]], axis=1))
        elif kind == 2:
            wts = (row2(norm_c[j]), w_in_c[j].astype(BF16), row2(qn_c[j]), row2(kn_c[j]))
            tail = (w_out_c[j].astype(BF16),) + pe
            q, k, v, sg, vb, km, kt = _proj_c(xp, wts, cos_p, sin_p, batch=bsz, seq=seq)
            y = _attn_prompt(q, kt, vb, sg, km.reshape(-1, WIDTH_ATT), batch=bsz, seq=seq)
            xp = _out_pe(xp, y, pp, i, tail, rows=TD)
            outs["kc_p"].append(k.reshape(bsz, seq, N_HEADS, HEAD_DIM))
            outs["vc_p"].append(v.reshape(bsz, seq, N_HEADS, HEAD_DIM))
            q, k, v, sg = _proj_c(xs, wts, cos_s, sin_s, batch=nsmp, seq=1)
            y = _attn_sample(q, k, v, sg, None, (cache_k_c, cache_v_c), j, page_table)
            xs = _out_pe(xs, y, ps, i, tail, rows=nsmp)
            outs["kc_s"].append(k.reshape(nsmp, 1, N_HEADS, HEAD_DIM))
            outs["vc_s"].append(v.reshape(nsmp, 1, N_HEADS, HEAD_DIM))
        else:
            wf = jnp.pad(w_in_d[j][:, 4 * WIDTH_ATT:], ((0, 0), (0, LANES - N_HEADS))).astype(BF16)
            bf = jnp.pad(b_f_d[j], (0, LANES - N_HEADS))
            wts = (row2(norm_d[j]), w_in_d[j][:, :4 * WIDTH_ATT].astype(BF16), wf, row2(bf),
                   jnp.transpose(wf), bf.reshape(LANES, 1), row2(qn_d[j]), row2(kn_d[j]))
            tail = (w_out_d[j].astype(BF16),) + pe
            q, k, v, sg, lf, vb, kt = _proj_d(xp, wts, batch=bsz, seq=seq)
            y = _attn_prompt(q, kt, vb, sg, None, batch=bsz, seq=seq)
            xp = _out_pe(xp, y, pp, i, tail, rows=TD)
            outs["kd_p"].append(k.reshape(bsz, seq, N_HEADS, HEAD_DIM))
            outs["vd_p"].append(v.reshape(bsz, seq, N_HEADS, HEAD_DIM))
            outs["lf_p"].append(lf.reshape(bsz, seq, N_HEADS))
            q, k, v, sg, lf, lft = _proj_d(xs, wts, batch=nsmp, seq=1)
            y = _attn_sample(q, k, v, sg, lft, (cache_k_d, cache_v_d, jnp.swapaxes(cache_logf_d, 2, 3)), j, page_table)
            xs = _out_pe(xs, y, ps, i, tail, rows=nsmp)
            outs["kd_s"].append(k.reshape(nsmp, 1, N_HEADS, HEAD_DIM))
            outs["vd_s"].append(v.reshape(nsmp, 1, N_HEADS, HEAD_DIM))
            outs["lf_s"].append(lf.reshape(nsmp, 1, N_HEADS))

    st = lambda key: jnp.stack(outs[key])
    return (xp.reshape(bsz, seq, D_MODEL), xs.reshape(nsmp, 1, D_MODEL),
            st("va_p"), st("va_s"), st("cb_p"), st("cb_s"),
            st("kc_p"), st("vc_p"), st("kc_s"), st("vc_s"),
            st("kd_p"), st("vd_p"), st("lf_p"), st("kd_s"), st("vd_s"), st("lf_s"))
```

```python
import functools
import math

import jax
import jax.numpy as jnp
import numpy as np
from jax import lax
from jax.experimental import pallas as pl
from jax.experimental.pallas import tpu as pltpu

F32 = jnp.float32
BF16 = jnp.bfloat16

D_MODEL = 1024
PLE_DIM = 256
WIDTH_A = 2048
GROUPS_A = 8
CHUNK_A = 128
WIDTH_B = 2048
CONV_W = 31
N_HEADS = 8
HEAD_DIM = 128
WIDTH_ATT = N_HEADS * HEAD_DIM
ROPE_DIM = HEAD_DIM // 4
ROPE_THETA = 500000.0
MOBA_BLOCK = 256
MOBA_TOPK = 3
PAGE_SIZE = 128
EPS = 1e-6

LANES = 128
TM = 256
TD = 512
CONV_PAD = 32
NEG = -1e30
SCALE = HEAD_DIM ** -0.5
VMEM_LIMIT = 60 * 1024 * 1024


def _cparams(*sem):
    return pltpu.CompilerParams(dimension_semantics=sem, vmem_limit_bytes=VMEM_LIMIT)


def _const_spec(shape):
    nd = len(shape)
    return pl.BlockSpec(shape, lambda *_: (0,) * nd, pipeline_mode=pl.Buffered(1))


def _dot(a, b):
    return jnp.dot(a, b, preferred_element_type=F32)


def _dot_nt(a, b):
    return lax.dot_general(a, b, (((1,), (1,)), ((), ())), preferred_element_type=F32)


def _sigmoid(x):
    return 1.0 / (1.0 + jnp.exp(-x))


def _silu(x):
    return x * _sigmoid(x)


def _gelu(x):
    return 0.5 * x * (1.0 + jnp.tanh(math.sqrt(2.0 / math.pi) * (x + 0.044715 * (x * x * x))))


def _rms(x, g):
    return x * lax.rsqrt(jnp.mean(x * x, axis=-1, keepdims=True) + EPS) * g


def _layer_norm(x, g, b):
    xc = x - jnp.mean(x, axis=-1, keepdims=True)
    return xc * lax.rsqrt(jnp.mean(xc * xc, axis=-1, keepdims=True) + EPS) * g + b


def _log_sigmoid(x):
    return jnp.minimum(x, 0.0) - jnp.log(1.0 + jnp.exp(-jnp.abs(x)))


def _split3(x):
    hi = x.astype(BF16)
    r = x - hi.astype(F32)
    mid = r.astype(BF16)
    lo = (r - mid.astype(F32)).astype(BF16)
    return hi, mid, lo


def _pe_update(x1, p_ref, peg_ref, wpg_ref, wpp_ref):
    xn = _rms(x1, peg_ref[...]).astype(BF16)
    gate = _sigmoid(_dot(xn, wpg_ref[...]))
    return x1 + gate * _dot(p_ref[...].astype(BF16), wpp_ref[...])


def _layer_a_kernel(x_ref, p_ref, ng_ref, win_ref, lng_ref, lnb_ref, ws_ref, bs_ref, wout_ref,
                    peg_ref, wpg_ref, wpp_ref, xo_ref, vo_ref, *scratch, chunked):
    x = x_ref[...]
    rows = x.shape[0]
    xn = _rms(x, ng_ref[...]).astype(BF16)
    w = WIDTH_A
    u = _gelu(_dot(xn, win_ref[:, 0:w]))
    v = _layer_norm(_gelu(_dot(xn, win_ref[:, w:2 * w])), lng_ref[...], lnb_ref[...])
    sg = _silu(_dot(xn, win_ref[:, 2 * w:3 * w]))
    if chunked:
        (s_ref,) = scratch
        vo_ref[0] = v[rows - CHUNK_A:, :]
        vb = v.astype(BF16)
        gw = WIDTH_A // GROUPS_A
        for c in range(rows // CHUNK_A):
            rs = slice(c * CHUNK_A, (c + 1) * CHUNK_A)
            for g in range(GROUPS_A):
                cs = slice(g * gw, (g + 1) * gw)
                s_ref[rs, cs] = _dot(ws_ref[g], vb[rs, cs]) + bs_ref[:, cs]
        s = s_ref[...]
    else:
        vo_ref[...] = v
        s = v * ws_ref[...] + bs_ref[...]
    y = (u * s * sg).astype(BF16)
    x1 = x + _dot(y, wout_ref[...])
    xo_ref[...] = _pe_update(x1, p_ref, peg_ref, wpg_ref, wpp_ref)


def _p_spec(rows, row, layer):
    return pl.BlockSpec((None, rows, PLE_DIM), lambda *g: (layer,) + row(*g))


def _layer_a(x, p, layer, wts, *, batch, seq):
    n = x.shape[0]
    chunked = seq > 1
    ng, win, lng, lnb, ws, bs, wout, peg, wpg, wpp = wts
    if chunked:
        tiles = seq // TD
        grid = (batch, tiles)
        row = lambda b, t: (b * tiles + t, 0)
        rows = TD
        vo_shape = jax.ShapeDtypeStruct((batch, CHUNK_A, WIDTH_A), F32)
        vo_spec = pl.BlockSpec((1, CHUNK_A, WIDTH_A), lambda b, t: (b, 0, 0))
        scratch = [pltpu.VMEM((TD, WIDTH_A), F32)]
    else:
        grid = (1, 1)
        row = lambda b, t: (0, 0)
        rows = n
        vo_shape = jax.ShapeDtypeStruct((n, WIDTH_A), F32)
        vo_spec = pl.BlockSpec((n, WIDTH_A), row)
        scratch = []
    consts = [ng, win, lng, lnb, ws, bs, wout, peg, wpg, wpp]
    return pl.pallas_call(
        functools.partial(_layer_a_kernel, chunked=chunked),
        out_shape=(jax.ShapeDtypeStruct((n, D_MODEL), F32), vo_shape),
        grid=grid,
        in_specs=[pl.BlockSpec((rows, D_MODEL), row), _p_spec(rows, row, layer)]
        + [_const_spec(c.shape) for c in consts],
        out_specs=(pl.BlockSpec((rows, D_MODEL), row), vo_spec),
        scratch_shapes=scratch,
        compiler_params=_cparams("arbitrary", "arbitrary"),
        name="layer_a_prompt" if chunked else "layer_a_sample",
    )(x, p, *consts)


CONV_ROWS = 64


def _layer_b_prompt_kernel(x_ref, p_ref, ng_ref, wab_ref, wg_ref, cw_ref, cb_ref, lng_ref, lnb_ref, wout_ref,
                           peg_ref, wpg_ref, wpp_ref, xo_ref, st_ref, hext_ref, c_ref):
    t = pl.program_id(1)
    ncb = WIDTH_B // LANES
    npair = wab_ref.shape[0] // 2
    rows = x_ref.shape[0]

    @pl.when(t == 0)
    def _():
        hext_ref[:, 0:CONV_PAD, :] = jnp.zeros((ncb, CONV_PAD, LANES), F32)

    x = x_ref[...]
    xn = _rms(x, ng_ref[...]).astype(BF16)

    def glu_cols(c):
        h2 = _dot(xn, wab_ref[c]) * _sigmoid(_dot(xn, wab_ref[npair + c]))
        hext_ref[2 * c, CONV_PAD:CONV_PAD + rows, :] = h2[:, 0:LANES]
        hext_ref[2 * c + 1, CONV_PAD:CONV_PAD + rows, :] = h2[:, LANES:2 * LANES]

    base = CONV_PAD - (CONV_W - 1)

    def conv_cols(cb):
        for rg in range(rows // CONV_ROWS):
            accs = [None] * (CONV_ROWS // 8)
            for k in range(CONV_W):
                wk = cw_ref[cb, k:k + 1, :]
                for j in range(CONV_ROWS // 8):
                    r0 = rg * CONV_ROWS + j * 8 + base + k
                    term = wk * hext_ref[cb, r0:r0 + 8, :]
                    accs[j] = term if accs[j] is None else accs[j] + term
            for j in range(CONV_ROWS // 8):
                r0 = rg * CONV_ROWS + j * 8
                c_ref[cb, r0:r0 + 8, :] = accs[j]

    glu_cols(0)

    def body(c, carry):
        conv_cols(2 * c)
        conv_cols(2 * c + 1)
        glu_cols(c + 1)
        return carry

    lax.fori_loop(0, npair - 1, body, 0)
    conv_cols(ncb - 2)
    conv_cols(ncb - 1)

    sg = _silu(_dot(xn, wg_ref[...]))
    for cb in range(ncb):
        st_ref[0, :, cb * LANES:(cb + 1) * LANES] = hext_ref[cb, rows + base:rows + CONV_PAD, :]
        hext_ref[cb, 0:CONV_PAD, :] = hext_ref[cb, rows:rows + CONV_PAD, :]
    c = jnp.concatenate([c_ref[cb] for cb in range(ncb)], axis=-1) + cb_ref[...]
    c = _silu(_layer_norm(c, lng_ref[...], lnb_ref[...]))
    y = (c * sg).astype(BF16)
    x1 = x + _dot(y, wout_ref[...])
    xo_ref[...] = _pe_update(x1, p_ref, peg_ref, wpg_ref, wpp_ref)


def _layer_b_prompt(x, p, layer, wts, *, batch, seq):
    n = x.shape[0]
    tiles = seq // TD
    ng, win, cw, cb, lng, lnb, wout, peg, wpg, wpp = wts
    ncb = WIDTH_B // LANES
    cw3 = jnp.transpose(cw.reshape(CONV_W, ncb, LANES), (1, 0, 2))
    wab = jnp.transpose(win[:, 0:2 * WIDTH_B].reshape(D_MODEL, ncb, 2 * LANES), (1, 0, 2))
    consts = [ng, wab, win[:, 2 * WIDTH_B:], cw3, cb, lng, lnb, wout, peg, wpg, wpp]
    row = lambda b, t: (b * tiles + t, 0)
    return pl.pallas_call(
        _layer_b_prompt_kernel,
        out_shape=(jax.ShapeDtypeStruct((n, D_MODEL), F32),
                   jax.ShapeDtypeStruct((batch, CONV_W - 1, WIDTH_B), F32)),
        grid=(batch, tiles),
        in_specs=[pl.BlockSpec((TD, D_MODEL), row), _p_spec(TD, row, layer)]
        + [_const_spec(c.shape) for c in consts],
        out_specs=(pl.BlockSpec((TD, D_MODEL), row),
                   pl.BlockSpec((1, CONV_W - 1, WIDTH_B), lambda b, t: (b, 0, 0))),
        scratch_shapes=[pltpu.VMEM((ncb, TD + CONV_PAD, LANES), F32), pltpu.VMEM((ncb, TD, LANES), F32)],
        compiler_params=_cparams("arbitrary", "arbitrary"),
        name="layer_b_prompt",
    )(x, p, *consts)


SAMPLE_CONV_ROWS = 16


def _layer_b_sample_kernel(x_ref, p_ref, st_ref, ng_ref, win_ref, cw_ref, cb_ref, lng_ref, lnb_ref, wout_ref,
                           peg_ref, wpg_ref, wpp_ref, xo_ref, ho_ref, h_ref, sg_ref, c_ref):
    i = pl.program_id(0)
    w = WIDTH_B

    @pl.when(i == 0)
    def _():
        xn = _rms(x_ref[...], ng_ref[...]).astype(BF16)
        h = _dot(xn, win_ref[:, 0:w]) * _sigmoid(_dot(xn, win_ref[:, w:2 * w]))
        h_ref[...] = h
        ho_ref[...] = h
        sg_ref[...] = _silu(_dot(xn, win_ref[:, 2 * w:3 * w]))

    r0 = pl.multiple_of(i * SAMPLE_CONV_ROWS, SAMPLE_CONV_ROWS)
    acc = cb_ref[...] + cw_ref[CONV_W - 1:CONV_W, :] * h_ref[pl.ds(r0, SAMPLE_CONV_ROWS), :]
    for k in range(CONV_W - 1):
        acc = acc + cw_ref[k:k + 1, :] * st_ref[k]
    c_ref[pl.ds(r0, SAMPLE_CONV_ROWS), :] = acc

    @pl.when(i == pl.num_programs(0) - 1)
    def _():
        c = _silu(_layer_norm(c_ref[...], lng_ref[...], lnb_ref[...]))
        y = (c * sg_ref[...]).astype(BF16)
        x1 = x_ref[...] + _dot(y, wout_ref[...])
        xo_ref[...] = _pe_update(x1, p_ref, peg_ref, wpg_ref, wpp_ref)


def _layer_b_sample(x, p, layer, state, wts):
    n = x.shape[0]
    consts = list(wts)
    full = lambda i: (0, 0)
    return pl.pallas_call(
        _layer_b_sample_kernel,
        out_shape=(jax.ShapeDtypeStruct((n, D_MODEL), F32), jax.ShapeDtypeStruct((n, WIDTH_B), F32)),
        grid=(n // SAMPLE_CONV_ROWS,),
        in_specs=[pl.BlockSpec((n, D_MODEL), full), _p_spec(n, full, layer),
                  pl.BlockSpec((CONV_W - 1, SAMPLE_CONV_ROWS, WIDTH_B), lambda i: (0, i, 0))]
        + [_const_spec(c.shape) for c in consts],
        out_specs=(pl.BlockSpec((n, D_MODEL), full), pl.BlockSpec((n, WIDTH_B), full)),
        scratch_shapes=[pltpu.VMEM((n, WIDTH_B), F32)] * 3,
        compiler_params=_cparams("arbitrary"),
        name="layer_b_sample",
    )(x, p, state, *consts)


AUG = 2 * HEAD_DIM


def _head_rms(xh, g):
    return xh * lax.rsqrt(jnp.mean(xh * xh, axis=-1, keepdims=True) + EPS) * g


def _rope(xh, cos, sin):
    half = ROPE_DIM // 2
    lane = lax.broadcasted_iota(jnp.int32, xh.shape, 1)
    partner = jnp.where(lane < half, pltpu.roll(xh, LANES - half, 1), pltpu.roll(xh, half, 1))
    return xh * cos + partner * sin


def _proj_c_kernel(x_ref, ng_ref, win_ref, qn_ref, kn_ref, cos_ref, sin_ref,
                   q_ref, k_ref, v_ref, sg_ref, *rest, prompt):
    xn = _rms(x_ref[...], ng_ref[...]).astype(BF16)
    rows = xn.shape[0]
    w = WIDTH_ATT
    cos = cos_ref[...]
    sin = sin_ref[...]
    qf = _dot(xn, win_ref[:, 0:w])
    kf = _dot(xn, win_ref[:, w:2 * w])
    v = _dot(xn, win_ref[:, 2 * w:3 * w])
    v_ref[...] = v
    if prompt:
        vt_ref, km_ref, qt_ref = rest
        lane = lax.broadcasted_iota(jnp.int32, (rows, HEAD_DIM), 1)
        onehot = jnp.where(lane == pl.program_id(1), 1.0, 0.0).astype(BF16)
    for h in range(N_HEADS):
        hs = slice(h * HEAD_DIM, (h + 1) * HEAD_DIM)
        qh = _rope(_head_rms(qf[:, hs], qn_ref[...]), cos, sin)
        kh = _rope(_head_rms(kf[:, hs], kn_ref[...]), cos, sin)
        k_ref[:, hs] = kh
        if prompt:
            km_ref[0, :, hs] = jnp.mean(kh, axis=0, keepdims=True)
            q_ref[:, h * AUG:h * AUG + HEAD_DIM] = kh.astype(BF16)
            q_ref[:, h * AUG + HEAD_DIM:(h + 1) * AUG] = onehot
            qt_ref[0, 0, h] = jnp.transpose(qh).astype(BF16)
            vt_ref[0, 0, h] = jnp.transpose(v[:, hs]).astype(BF16)
        else:
            q_ref[:, hs] = qh.astype(BF16)
    sg_ref[...] = _silu(_dot(xn, win_ref[:, 3 * w:4 * w]))


def _proj_c(x, wts, cos, sin, *, batch, seq):
    n = x.shape[0]
    prompt = seq > 1
    rows = TM if prompt else n
    tiles = seq // rows if prompt else 1
    nb = batch if prompt else 1
    consts = list(wts)
    row = lambda b, t: (b * tiles + t, 0)
    pos = lambda b, t: (t, 0)
    wide = jax.ShapeDtypeStruct((n, WIDTH_ATT), F32)
    spec = pl.BlockSpec((rows, WIDTH_ATT), row)
    qw = N_HEADS * AUG if prompt else WIDTH_ATT
    out_shape = [jax.ShapeDtypeStruct((n, qw), BF16), wide, wide, wide]
    out_specs = [pl.BlockSpec((rows, qw), row), spec, spec, spec]
    if prompt:
        per_tile = lambda b, t: (b, t, 0, 0, 0)
        out_shape += [jax.ShapeDtypeStruct((nb, tiles, N_HEADS, HEAD_DIM, rows), BF16),
                      jax.ShapeDtypeStruct((nb * tiles, 1, WIDTH_ATT), F32),
                      jax.ShapeDtypeStruct((nb, tiles, N_HEADS, HEAD_DIM, rows), BF16)]
        out_specs += [pl.BlockSpec((1, 1, N_HEADS, HEAD_DIM, rows), per_tile),
                      pl.BlockSpec((1, 1, WIDTH_ATT), lambda b, t: (b * tiles + t, 0, 0)),
                      pl.BlockSpec((1, 1, N_HEADS, HEAD_DIM, rows), per_tile)]
    return pl.pallas_call(
        functools.partial(_proj_c_kernel, prompt=prompt),
        out_shape=tuple(out_shape),
        grid=(nb, tiles),
        in_specs=[pl.BlockSpec((rows, D_MODEL), row)] + [_const_spec(c.shape) for c in consts]
        + [pl.BlockSpec((rows, HEAD_DIM), pos), pl.BlockSpec((rows, HEAD_DIM), pos)],
        out_specs=tuple(out_specs),
        compiler_params=_cparams("arbitrary", "arbitrary"),
        name="proj_c_prompt" if prompt else "proj_c_sample",
    )(x, *consts, cos, sin)


def _proj_d_kernel(x_ref, ng_ref, win_ref, wf_ref, bf_ref, wft_ref, bft_ref, qn_ref, kn_ref,
                   q_ref, k_ref, v_ref, sg_ref, lf_ref, *rest, prompt):
    xn = _rms(x_ref[...], ng_ref[...]).astype(BF16)
    rows = xn.shape[0]
    w = WIDTH_ATT
    qf = _dot(xn, win_ref[:, 0:w])
    kf = _dot(xn, win_ref[:, w:2 * w])
    v = _dot(xn, win_ref[:, 2 * w:3 * w])
    v_ref[...] = v
    sg_ref[...] = _silu(_dot(xn, win_ref[:, 3 * w:4 * w]))
    lane = lax.broadcasted_iota(jnp.int32, (rows, LANES), 1)
    logf = jnp.where(lane < N_HEADS, _log_sigmoid(_dot(xn, wf_ref[...]) + bf_ref[...]), 0.0)
    lf_ref[...] = logf[:, 0:N_HEADS]
    if not prompt:
        (lft_ref,) = rest
        for h in range(N_HEADS):
            hs = slice(h * HEAD_DIM, (h + 1) * HEAD_DIM)
            q_ref[:, hs] = _head_rms(qf[:, hs], qn_ref[...]).astype(BF16)
            k_ref[:, hs] = _head_rms(kf[:, hs], kn_ref[...])
        lft_ref[...] = _log_sigmoid(_dot_nt(wft_ref[...], xn) + bft_ref[...])
        return

    vt_ref, qt_ref, carry_ref = rest

    @pl.when(pl.program_id(1) == 0)
    def _():
        carry_ref[...] = jnp.zeros((1, LANES), F32)

    r = lax.broadcasted_iota(jnp.int32, (rows, rows), 0)
    c = lax.broadcasted_iota(jnp.int32, (rows, rows), 1)
    tri = jnp.where(c <= r, 1.0, 0.0).astype(BF16)
    hi, mid, lo = _split3(logf)
    cum = _dot(tri, hi) + _dot(tri, mid) + _dot(tri, lo) + carry_ref[...]
    carry_ref[...] = cum[rows - 1:rows, :]
    pieces = [p.astype(F32) for p in _split3(cum * (1.0 / SCALE))]
    pieces_t = [jnp.transpose(p) for p in pieces]
    sub = lax.broadcasted_iota(jnp.int32, (HEAD_DIM, rows), 0)
    for h in range(N_HEADS):
        hs = slice(h * HEAD_DIM, (h + 1) * HEAD_DIM)
        c1, c2, c3 = [p[h:h + 1, :] for p in pieces_t]
        aug_q = jnp.where(sub == 0, c1, jnp.where(sub == 1, c2, jnp.where(sub == 2, c3,
                          jnp.where(sub < 6, 1.0, 0.0))))
        qt_ref[0, 0, h, 0:HEAD_DIM, :] = jnp.transpose(_head_rms(qf[:, hs], qn_ref[...])).astype(BF16)
        qt_ref[0, 0, h, HEAD_DIM:AUG, :] = aug_q.astype(BF16)
        kh = _head_rms(kf[:, hs], kn_ref[...])
        k_ref[:, hs] = kh
        d1, d2, d3 = [p[:, h:h + 1] for p in pieces]
        aug_k = jnp.where(lane == 3, -d1, jnp.where(lane == 4, -d2, jnp.where(lane == 5, -d3,
                          jnp.where(lane < 3, 1.0, 0.0))))
        q_ref[:, h * AUG:h * AUG + HEAD_DIM] = kh.astype(BF16)
        q_ref[:, h * AUG + HEAD_DIM:(h + 1) * AUG] = aug_k.astype(BF16)
        vt_ref[0, 0, h] = jnp.transpose(v[:, hs]).astype(BF16)


def _proj_d(x, wts, *, batch, seq):
    n = x.shape[0]
    prompt = seq > 1
    rows = TM if prompt else n
    tiles = seq // rows if prompt else 1
    nb = batch if prompt else 1
    consts = list(wts)
    row = lambda b, t: (b * tiles + t, 0)
    wide = jax.ShapeDtypeStruct((n, WIDTH_ATT), F32)
    spec = pl.BlockSpec((rows, WIDTH_ATT), row)
    qw = N_HEADS * AUG if prompt else WIDTH_ATT
    out_shape = [jax.ShapeDtypeStruct((n, qw), BF16), wide, wide, wide, jax.ShapeDtypeStruct((n, N_HEADS), F32)]
    out_specs = [pl.BlockSpec((rows, qw), row), spec, spec, spec, pl.BlockSpec((rows, N_HEADS), row)]
    scratch = []
    if prompt:
        per_tile = lambda b, t: (b, t, 0, 0, 0)
        out_shape += [jax.ShapeDtypeStruct((nb, tiles, N_HEADS, HEAD_DIM, rows), BF16),
                      jax.ShapeDtypeStruct((nb, tiles, N_HEADS, AUG, rows), BF16)]
        out_specs += [pl.BlockSpec((1, 1, N_HEADS, HEAD_DIM, rows), per_tile),
                      pl.BlockSpec((1, 1, N_HEADS, AUG, rows), per_tile)]
        scratch = [pltpu.VMEM((1, LANES), F32)]
    else:
        out_shape.append(jax.ShapeDtypeStruct((LANES, n), F32))
        out_specs.append(pl.BlockSpec((LANES, n), lambda b, t: (0, 0)))
    return pl.pallas_call(
        functools.partial(_proj_d_kernel, prompt=prompt),
        out_shape=tuple(out_shape),
        grid=(nb, tiles),
        in_specs=[pl.BlockSpec((rows, D_MODEL), row)] + [_const_spec(c.shape) for c in consts],
        out_specs=tuple(out_specs),
        scratch_shapes=scratch,
        compiler_params=_cparams("arbitrary", "arbitrary"),
        name="proj_d_prompt" if prompt else "proj_d_sample",
    )(x, *consts)


EXP2_SCALE = SCALE * math.log2(math.e)


def _flash_block(j, qa, ka_ref, vt_ref, m_ref, l_ref, acc_ref, causal):
    blk = TM
    r0 = pl.multiple_of(j * blk, blk)
    scores = [_dot(ka_ref[pl.ds(r0, blk), h * AUG:(h + 1) * AUG], qa(h)) for h in range(N_HEADS)]
    probs, alphas = [], []
    for h in range(N_HEADS):
        s = scores[h]
        if causal is not None:
            s = jnp.where(causal, s, NEG)
        m_old = m_ref[h]
        m_new = jnp.maximum(m_old, jnp.max(s, axis=0, keepdims=True))
        alpha = jnp.exp2((m_old - m_new) * EXP2_SCALE)
        p = jnp.exp2((s - m_new) * EXP2_SCALE)
        l_ref[h] = alpha * l_ref[h] + jnp.sum(p, axis=0, keepdims=True)
        m_ref[h] = m_new
        probs.append(p.astype(BF16))
        alphas.append(alpha)
    for h in range(N_HEADS):
        acc_ref[h] = alphas[h] * acc_ref[h] + _dot(vt_ref[0, j, h], probs[h])


def _flash_all(i, qa, ka_ref, vt_ref, sg_ref, y_ref, m_ref, l_ref, acc_ref):
    blk = TM
    m_ref[...] = jnp.full(m_ref.shape, NEG, F32)
    l_ref[...] = jnp.zeros(l_ref.shape, F32)
    acc_ref[...] = jnp.zeros(acc_ref.shape, F32)

    def body(j, carry):
        _flash_block(j, qa, ka_ref, vt_ref, m_ref, l_ref, acc_ref, None)
        return carry

    lax.fori_loop(0, i, body, 0)
    key = lax.broadcasted_iota(jnp.int32, (blk, blk), 0)
    qry = lax.broadcasted_iota(jnp.int32, (blk, blk), 1)
    _flash_block(i, qa, ka_ref, vt_ref, m_ref, l_ref, acc_ref, key <= qry)
    for h in range(N_HEADS):
        hs = slice(h * HEAD_DIM, (h + 1) * HEAD_DIM)
        y_ref[:, hs] = (jnp.transpose(acc_ref[h] / l_ref[h]) * sg_ref[:, hs]).astype(BF16)


def _attn_c_prompt_kernel(qt_ref, ka_ref, vt_ref, km_ref, sg_ref, y_ref, qa_ref, m_ref, l_ref, acc_ref):
    i = pl.program_id(1)
    blk = MOBA_BLOCK
    nblk = km_ref.shape[0]
    sub = lax.broadcasted_iota(jnp.int32, (nblk, blk), 0)
    valid = sub < i
    for h in range(N_HEADS):
        hs = slice(h * HEAD_DIM, (h + 1) * HEAD_DIM)
        qt = qt_ref[0, 0, h]
        km = km_ref[:, hs]
        km_hi = km.astype(BF16)
        km_lo = (km - km_hi.astype(F32)).astype(BF16)
        gate = _dot(km_hi, qt) + _dot(km_lo, qt)
        bias = jnp.zeros((nblk, blk), F32)
        for n in range(nblk):
            gn = gate[n:n + 1, :]
            ahead = valid & ((gate > gn) | ((gate == gn) & (sub < n)))
            rank = jnp.sum(jnp.where(ahead, 1.0, 0.0), axis=0, keepdims=True)
            bias = jnp.where((sub == n) & (rank >= MOBA_TOPK) & valid, NEG, bias)
        bias = jnp.concatenate([bias, jnp.zeros((HEAD_DIM - nblk, blk), F32)], axis=0)
        qa_ref[h, 0:HEAD_DIM, :] = qt
        qa_ref[h, HEAD_DIM:AUG, :] = bias.astype(BF16)
    _flash_all(i, lambda h: qa_ref[h], ka_ref, vt_ref, sg_ref, y_ref, m_ref, l_ref, acc_ref)


def _attn_d_prompt_kernel(qt_ref, ka_ref, vt_ref, sg_ref, y_ref, m_ref, l_ref, acc_ref):
    i = pl.program_id(1)
    _flash_all(i, lambda h: qt_ref[0, 0, h], ka_ref, vt_ref, sg_ref, y_ref, m_ref, l_ref, acc_ref)


def _attn_prompt(qt, ka, vt, sg, km, *, batch, seq):
    n = ka.shape[0]
    tiles = seq // TM
    row = lambda b, t: (b * tiles + t, 0)
    per_b = lambda b, t: (b, 0)
    stat = pltpu.VMEM((N_HEADS, 1, TM), F32)
    in_specs = [pl.BlockSpec((1, 1) + qt.shape[2:], lambda b, t: (b, t, 0, 0, 0)),
                pl.BlockSpec((seq, N_HEADS * AUG), per_b),
                pl.BlockSpec((1, tiles, N_HEADS, HEAD_DIM, TM), lambda b, t: (b, 0, 0, 0, 0))]
    args = [qt, ka, vt]
    scratch = [stat, stat, pltpu.VMEM((N_HEADS, HEAD_DIM, TM), F32)]
    if km is not None:
        in_specs.append(pl.BlockSpec((tiles, WIDTH_ATT), per_b))
        args.append(km)
        scratch = [pltpu.VMEM((N_HEADS, AUG, TM), BF16)] + scratch
    in_specs.append(pl.BlockSpec((TM, WIDTH_ATT), row))
    args.append(sg)
    return pl.pallas_call(
        _attn_d_prompt_kernel if km is None else _attn_c_prompt_kernel,
        out_shape=jax.ShapeDtypeStruct((n, WIDTH_ATT), BF16),
        grid=(batch, tiles),
        in_specs=in_specs,
        out_specs=pl.BlockSpec((TM, WIDTH_ATT), row),
        scratch_shapes=scratch,
        compiler_params=_cparams("arbitrary", "arbitrary"),
        name="attn_d_prompt" if km is None else "attn_c_prompt",
    )(*args)


LOG2E = math.log2(math.e)


def _page_scores(kp, qs, bias=None):
    prod = kp * qs[None]
    if bias is not None:
        prod = prod + bias
    return jnp.sum(prod, axis=-1, keepdims=True)


def _attn_c_sample_kernel(pt_ref, q_ref, kn_ref, vn_ref, sg_ref, *rest, npages):
    k_refs, v_refs = rest[:npages], rest[npages:2 * npages]
    y_ref = rest[2 * npages]
    q = q_ref[0].astype(F32)
    qs = q * (SCALE * LOG2E)
    ppb = MOBA_BLOCK // PAGE_SIZE
    nblk = npages // ppb
    gates, ms, ls, accs = [], [], [], []
    for n in range(nblk):
        kps = [k_refs[ppb * n + r][0, 0] for r in range(ppb)]
        vps = [v_refs[ppb * n + r][0, 0] for r in range(ppb)]
        ss = [_page_scores(kp, qs) for kp in kps]
        ksum = functools.reduce(lambda a, b: a + b, [jnp.sum(kp, axis=0) for kp in kps])
        gates.append(jnp.sum(q * (ksum * (1.0 / MOBA_BLOCK)), axis=-1, keepdims=True))
        m = functools.reduce(jnp.maximum, [jnp.max(s, axis=0) for s in ss])
        es = [jnp.exp2(s - m[None]) for s in ss]
        ms.append(m)
        ls.append(functools.reduce(lambda a, b: a + b, [jnp.sum(e, axis=0) for e in es]))
        accs.append(functools.reduce(lambda a, b: a + b, [jnp.sum(e * vp, axis=0) for e, vp in zip(es, vps)]))
    sels = []
    for n in range(nblk):
        rank = jnp.zeros((N_HEADS, 1), F32)
        for mth in range(nblk):
            if mth == n:
                continue
            ahead = (gates[mth] >= gates[n]) if mth < n else (gates[mth] > gates[n])
            rank = rank + jnp.where(ahead, 1.0, 0.0)
        sels.append(rank < MOBA_TOPK)
    s_self = jnp.sum(qs * kn_ref[0], axis=-1, keepdims=True)
    mx = s_self
    for n in range(nblk):
        mx = jnp.maximum(mx, jnp.where(sels[n], ms[n], NEG))
    wself = jnp.exp2(s_self - mx)
    l = wself
    acc = wself * vn_ref[0]
    for n in range(nblk):
        wgt = jnp.where(sels[n], jnp.exp2(ms[n] - mx), 0.0)
        l = l + wgt * ls[n]
        acc = acc + wgt * accs[n]
    y_ref[0] = ((acc / l) * sg_ref[0]).astype(BF16)


def _attn_d_sample_kernel(pt_ref, q_ref, kn_ref, vn_ref, sg_ref, lft_ref, *rest, npages):
    k_refs, v_refs, lf_refs = rest[:npages], rest[npages:2 * npages], rest[2 * npages:3 * npages]
    y_ref = rest[3 * npages]
    b = pl.program_id(0)
    q = q_ref[0].astype(F32)
    qs = q * (SCALE * LOG2E)
    blocks = [lf_refs[pg][0, 0] for pg in range(npages)]
    if npages * N_HEADS < LANES:
        blocks.append(jnp.zeros((LANES - npages * N_HEADS, PAGE_SIZE), F32))
    lft = jnp.concatenate(blocks, axis=0)
    r = lax.broadcasted_iota(jnp.int32, (LANES, LANES), 0)
    c = lax.broadcasted_iota(jnp.int32, (LANES, LANES), 1)
    after = jnp.where(r > c, 1.0, 0.0).astype(BF16)
    within = sum(_dot(p, after) for p in _split3(lft))
    tot = jnp.broadcast_to(within[:, 0:1] + lft[:, 0:1], (LANES, LANES))
    later = jnp.where((r % N_HEADS == c % N_HEADS) & (c // N_HEADS > r // N_HEADS), 1.0, 0.0).astype(BF16)
    carry = sum(_dot(later, p) for p in _split3(tot))
    lane = lax.broadcasted_iota(jnp.int32, (N_HEADS, lft_ref.shape[1]), 1)
    lf_new = jnp.sum(jnp.where(lane == b, lft_ref[0:N_HEADS, :], 0.0), axis=-1, keepdims=True)
    dec = (within + carry) * LOG2E
    key = lax.broadcasted_iota(jnp.int32, (PAGE_SIZE, N_HEADS, LANES), 0)
    ln3 = lax.broadcasted_iota(jnp.int32, (PAGE_SIZE, N_HEADS, LANES), 2)
    diag = jnp.where(key == ln3, 1.0, 0.0)

    ms, ls, accs = [], [], []
    for pg in range(npages):
        dec_pg = dec[pg * N_HEADS:(pg + 1) * N_HEADS, :] + lf_new * LOG2E
        s = _page_scores(k_refs[pg][0, 0], qs, diag * dec_pg[None])
        m = jnp.max(s, axis=0)
        e = jnp.exp2(s - m[None])
        ms.append(m)
        ls.append(jnp.sum(e, axis=0))
        accs.append(jnp.sum(e * v_refs[pg][0, 0], axis=0))
    s_self = jnp.sum(qs * kn_ref[0], axis=-1, keepdims=True)
    mx = functools.reduce(jnp.maximum, ms, s_self)
    wself = jnp.exp2(s_self - mx)
    l = wself
    acc = wself * vn_ref[0]
    for pg in range(npages):
        wgt = jnp.exp2(ms[pg] - mx)
        l = l + wgt * ls[pg]
        acc = acc + wgt * accs[pg]
    y_ref[0] = ((acc / l) * sg_ref[0]).astype(BF16)


def _attn_sample(q, k_new, v_new, sg, lft, caches, layer, page_table):
    n = q.shape[0]
    npages = page_table.shape[1]
    r3 = lambda a: a.reshape(n, N_HEADS, HEAD_DIM)
    rowspec = pl.BlockSpec((1, N_HEADS, HEAD_DIM), lambda b, pt: (b, 0, 0))
    in_specs = [rowspec] * 4
    args = [r3(q), r3(k_new), r3(v_new), r3(sg)]
    if lft is not None:
        in_specs.append(pl.BlockSpec(lft.shape, lambda b, pt: (0, 0)))
        args.append(lft)
    for cache in caches:
        blk = (1, 1) + cache.shape[2:]
        tail = (0,) * (cache.ndim - 2)
        for pg in range(npages):
            in_specs.append(pl.BlockSpec(blk, lambda b, pt, pg=pg, tail=tail: (layer, pt[b * npages + pg]) + tail))
            args.append(cache)
    body = _attn_c_sample_kernel if lft is None else _attn_d_sample_kernel
    y = pl.pallas_call(
        functools.partial(body, npages=npages),
        out_shape=jax.ShapeDtypeStruct((n, N_HEADS, HEAD_DIM), BF16),
        grid_spec=pltpu.PrefetchScalarGridSpec(
            num_scalar_prefetch=1, grid=(n,), in_specs=in_specs, out_specs=rowspec),
        compiler_params=_cparams("arbitrary"),
        name="attn_c_sample" if lft is None else "attn_d_sample",
    )(page_table.reshape(-1), *args)
    return y.reshape(n, WIDTH_ATT)


def _out_pe_kernel(x_ref, y_ref, p_ref, wout_ref, peg_ref, wpg_ref, wpp_ref, xo_ref):
    x1 = x_ref[...] + _dot(y_ref[...], wout_ref[...])
    xo_ref[...] = _pe_update(x1, p_ref, peg_ref, wpg_ref, wpp_ref)


def _out_pe(x, y, p, layer, wts, *, rows):
    n = x.shape[0]
    consts = list(wts)
    row = lambda i: (i, 0)
    return pl.pallas_call(
        _out_pe_kernel,
        out_shape=jax.ShapeDtypeStruct((n, D_MODEL), F32),
        grid=(n // rows,),
        in_specs=[pl.BlockSpec((rows, D_MODEL), row), pl.BlockSpec((rows, WIDTH_ATT), row),
                  _p_spec(rows, row, layer)] + [_const_spec(c.shape) for c in consts],
        out_specs=pl.BlockSpec((rows, D_MODEL), row),
        compiler_params=_cparams("arbitrary"),
        name="out_pe",
    )(x, y, p, *consts)


def _rope_tables(pos):
    half = ROPE_DIM // 2
    inv = ROPE_THETA ** (-jnp.arange(half, dtype=F32) * 2.0 / ROPE_DIM)
    ang = pos.astype(F32)[:, None] * inv[None, :]
    cos, sin = jnp.cos(ang), jnp.sin(ang)
    rest = HEAD_DIM - ROPE_DIM
    ones = jnp.ones((pos.shape[0], rest), F32)
    zeros = jnp.zeros((pos.shape[0], rest), F32)
    return jnp.concatenate([cos, cos, ones], axis=-1), jnp.concatenate([-sin, sin, zeros], axis=-1)


def kernel(x_prompt, x_sample, state_conv_b, cache_k_c, cache_v_c, cache_k_d, cache_v_d, cache_logf_d, page_table, p_prompt, p_sample, norm_a, w_in_a, ln_g_a, ln_b_a, w_s_a, b_s_a, w_out_a, norm_b, w_in_b, conv_w_b, conv_b_b, ln_g_b, ln_b_b, w_out_b, norm_c, w_in_c, qn_c, kn_c, w_out_c, norm_d, w_in_d, b_f_d, qn_d, kn_d, w_out_d, pe_norm, w_pe_gate, w_pe_proj):
    bsz, seq, _ = x_prompt.shape
    nsmp = x_sample.shape[0]
    depth = pe_norm.shape[0]
    past_len = page_table.shape[1] * PAGE_SIZE
    row2 = lambda a: a.reshape(1, -1)

    pp = p_prompt.reshape(depth, bsz * seq, PLE_DIM)
    ps = p_sample.reshape(depth, nsmp, PLE_DIM)
    xp = x_prompt.reshape(bsz * seq, D_MODEL)
    xs = x_sample.reshape(nsmp, D_MODEL)
    cos_p, sin_p = _rope_tables(jnp.arange(seq))
    cos_s, sin_s = _rope_tables(jnp.full((nsmp,), past_len))

    outs = {k: [] for k in ("va_p", "va_s", "cb_p", "cb_s", "kc_p", "vc_p", "kc_s", "vc_s",
                            "kd_p", "vd_p", "lf_p", "kd_s", "vd_s", "lf_s")}
    for i in range(depth):
        j, kind = divmod(i, 4)
        pe = (row2(pe_norm[i]), w_pe_gate[i].astype(BF16), w_pe_proj[i].astype(BF16))
        if kind == 0:
            gw = WIDTH_A // GROUPS_A
            base = (row2(norm_a[j]), w_in_a[j].astype(BF16), row2(ln_g_a[j]), row2(ln_b_a[j]))
            tail = (w_out_a[j].astype(BF16),) + pe
            ws_p = jnp.tril(w_s_a[j]).astype(BF16)
            bs_p = jnp.repeat(jnp.transpose(b_s_a[j]), gw, axis=1)
            xp, va = _layer_a(xp, pp, i, base + (ws_p, bs_p) + tail, batch=bsz, seq=seq)
            ws_s = row2(jnp.repeat(w_s_a[j][:, 0, 0], gw))
            bs_s = row2(jnp.repeat(b_s_a[j][:, 0], gw))
            xs, vs = _layer_a(xs, ps, i, base + (ws_s, bs_s) + tail, batch=nsmp, seq=1)
            outs["va_p"].append(va)
            outs["va_s"].append(vs.reshape(nsmp, 1, WIDTH_A))
        elif kind == 1:
            wts = (row2(norm_b[j]), w_in_b[j].astype(BF16), conv_w_b[j], row2(conv_b_b[j]),
                   row2(ln_g_b[j]), row2(ln_b_b[j]), w_out_b[j].astype(BF16)) + pe
            xp, cb = _layer_b_prompt(xp, pp, i, wts, batch=bsz, seq=seq)
            xs, hs = _layer_b_sample(xs, ps, i, jnp.transpose(state_conv_b[j], (1, 0, 2)), wts)
            outs["cb_p"].append(cb)
            outs["cb_s"].append(jnp.concatenate([state_conv_b[j][:, 1:], hs[:, None, :]], axis=1))
        elif kind == 2:
            wts = (row2(norm_c[j]), w_in_c[j].astype(BF16), row2(qn_c[j]), row2(kn_c[j]))
            tail = (w_out_c[j].astype(BF16),) + pe
            ka, k, v, sg, vt, km, qt = _proj_c(xp, wts, cos_p, sin_p, batch=bsz, seq=seq)
            y = _attn_prompt(qt, ka, vt, sg, km.reshape(-1, WIDTH_ATT), batch=bsz, seq=seq)
            xp = _out_pe(xp, y, pp, i, tail, rows=TD)
            outs["kc_p"].append(k.reshape(bsz, seq, N_HEADS, HEAD_DIM))
            outs["vc_p"].append(v.reshape(bsz, seq, N_HEADS, HEAD_DIM))
            q, k, v, sg = _proj_c(xs, wts, cos_s, sin_s, batch=nsmp, seq=1)
            y = _attn_sample(q, k, v, sg, None, (cache_k_c, cache_v_c), j, page_table)
            xs = _out_pe(xs, y, ps, i, tail, rows=nsmp)
            outs["kc_s"].append(k.reshape(nsmp, 1, N_HEADS, HEAD_DIM))
            outs["vc_s"].append(v.reshape(nsmp, 1, N_HEADS, HEAD_DIM))
        else:
            wf = jnp.pad(w_in_d[j][:, 4 * WIDTH_ATT:], ((0, 0), (0, LANES - N_HEADS))).astype(BF16)
            bf = jnp.pad(b_f_d[j], (0, LANES - N_HEADS))
            wts = (row2(norm_d[j]), w_in_d[j][:, :4 * WIDTH_ATT].astype(BF16), wf, row2(bf),
                   jnp.transpose(wf), bf.reshape(LANES, 1), row2(qn_d[j]), row2(kn_d[j]))
            tail = (w_out_d[j].astype(BF16),) + pe
            ka, k, v, sg, lf, vt, qt = _proj_d(xp, wts, batch=bsz, seq=seq)
            y = _attn_prompt(qt, ka, vt, sg, None, batch=bsz, seq=seq)
            xp = _out_pe(xp, y, pp, i, tail, rows=TD)
            outs["kd_p"].append(k.reshape(bsz, seq, N_HEADS, HEAD_DIM))
            outs["vd_p"].append(v.reshape(bsz, seq, N_HEADS, HEAD_DIM))
            outs["lf_p"].append(lf.reshape(bsz, seq, N_HEADS))
            q, k, v, sg, lf, lft = _proj_d(xs, wts, batch=nsmp, seq=1)
            y = _attn_sample(q, k, v, sg, lft, (cache_k_d, cache_v_d, jnp.swapaxes(cache_logf_d, 2, 3)), j, page_table)
            xs = _out_pe(xs, y, ps, i, tail, rows=nsmp)
            outs["kd_s"].append(k.reshape(nsmp, 1, N_HEADS, HEAD_DIM))
            outs["vd_s"].append(v.reshape(nsmp, 1, N_HEADS, HEAD_DIM))
            outs["lf_s"].append(lf.reshape(nsmp, 1, N_HEADS))

    st = lambda key: jnp.stack(outs[key])
    return (xp.reshape(bsz, seq, D_MODEL), xs.reshape(nsmp, 1, D_MODEL),
            st("va_p"), st("va_s"), st("cb_p"), st("cb_s"),
            st("kc_p"), st("vc_p"), st("kc_s"), st("vc_s"),
            st("kd_p"), st("vd_p"), st("lf_p"), st("kd_s"), st("vd_s"), st("lf_s"))
```

```python
import functools
import math

import jax
import jax.numpy as jnp
import numpy as np
from jax import lax
from jax.experimental import pallas as pl
from jax.experimental.pallas import tpu as pltpu

F32 = jnp.float32
BF16 = jnp.bfloat16

D_MODEL = 1024
PLE_DIM = 256
WIDTH_A = 2048
GROUPS_A = 8
CHUNK_A = 128
WIDTH_B = 2048
CONV_W = 31
N_HEADS = 8
HEAD_DIM = 128
WIDTH_ATT = N_HEADS * HEAD_DIM
ROPE_DIM = HEAD_DIM // 4
ROPE_THETA = 500000.0
MOBA_BLOCK = 256
MOBA_TOPK = 3
PAGE_SIZE = 128
EPS = 1e-6

LANES = 128
TM = 256
TD = 512
CONV_PAD = 32
NEG = -1e30
SCALE = HEAD_DIM ** -0.5
VMEM_LIMIT = 60 * 1024 * 1024


def _cparams(*sem):
    return pltpu.CompilerParams(dimension_semantics=sem, vmem_limit_bytes=VMEM_LIMIT)


def _const_spec(shape):
    nd = len(shape)
    return pl.BlockSpec(shape, lambda *_: (0,) * nd, pipeline_mode=pl.Buffered(1))


def _dot(a, b):
    return jnp.dot(a, b, preferred_element_type=F32)


def _dot_nt(a, b):
    return lax.dot_general(a, b, (((1,), (1,)), ((), ())), preferred_element_type=F32)


def _sigmoid(x):
    return 1.0 / (1.0 + jnp.exp(-x))


def _silu(x):
    return x * _sigmoid(x)


def _gelu(x):
    return 0.5 * x * (1.0 + jnp.tanh(math.sqrt(2.0 / math.pi) * (x + 0.044715 * (x * x * x))))


def _rms(x, g):
    return x * lax.rsqrt(jnp.mean(x * x, axis=-1, keepdims=True) + EPS) * g


def _layer_norm(x, g, b):
    xc = x - jnp.mean(x, axis=-1, keepdims=True)
    return xc * lax.rsqrt(jnp.mean(xc * xc, axis=-1, keepdims=True) + EPS) * g + b


def _log_sigmoid(x):
    return jnp.minimum(x, 0.0) - jnp.log(1.0 + jnp.exp(-jnp.abs(x)))


def _split3(x):
    hi = x.astype(BF16)
    r = x - hi.astype(F32)
    mid = r.astype(BF16)
    lo = (r - mid.astype(F32)).astype(BF16)
    return hi, mid, lo


def _pe_update(x1, p_ref, peg_ref, wpg_ref, wpp_ref):
    xn = _rms(x1, peg_ref[...]).astype(BF16)
    gate = _sigmoid(_dot(xn, wpg_ref[...]))
    return x1 + gate * _dot(p_ref[...].astype(BF16), wpp_ref[...])


def _row_parts(rows, n):
    return [slice(i * (rows // n), (i + 1) * (rows // n)) for i in range(n)]


def _pe_store(xo_ref, parts, x1s, p_ref, peg_ref, wpg_ref, wpp_ref):
    xns = [_rms(x1, peg_ref[...]).astype(BF16) for x1 in x1s]
    gates = [_sigmoid(_dot(xn, wpg_ref[...])) for xn in xns]
    projs = [_dot(p_ref[ps, :].astype(BF16), wpp_ref[...]) for ps in parts]
    for ps, x1, gate, proj in zip(parts, x1s, gates, projs):
        xo_ref[ps, :] = x1 + gate * proj


def _layer_a_kernel(x_ref, p_ref, ng_ref, win_ref, lng_ref, lnb_ref, ws_ref, bs_ref, wout_ref,
                    peg_ref, wpg_ref, wpp_ref, xo_ref, vo_ref, *scratch, chunked):
    rows = x_ref.shape[0]
    parts = _row_parts(rows, 2 if chunked else 1)
    w = WIDTH_A
    xs = [x_ref[ps, :] for ps in parts]
    xns = [_rms(x, ng_ref[...]).astype(BF16) for x in xs]
    hv = [_dot(xn, win_ref[:, w:2 * w]) for xn in xns]
    hu = [_dot(xn, win_ref[:, 0:w]) for xn in xns]
    hg = [_dot(xn, win_ref[:, 2 * w:3 * w]) for xn in xns]
    vs = [_layer_norm(_gelu(h), lng_ref[...], lnb_ref[...]) for h in hv]
    if chunked:
        (s_ref,) = scratch
        vo_ref[0] = vs[-1][parts[-1].stop - parts[-1].start - CHUNK_A:, :]
        gw = WIDTH_A // GROUPS_A
        for ps, v in zip(parts, vs):
            vb = v.astype(BF16)
            for c in range((ps.stop - ps.start) // CHUNK_A):
                rs = slice(c * CHUNK_A, (c + 1) * CHUNK_A)
                for g in range(GROUPS_A):
                    cs = slice(g * gw, (g + 1) * gw)
                    s_ref[ps.start + c * CHUNK_A:ps.start + (c + 1) * CHUNK_A, cs] = (
                        _dot(ws_ref[g], vb[rs, cs]) + bs_ref[:, cs])
        ss = [s_ref[ps, :] for ps in parts]
    else:
        vo_ref[...] = vs[0]
        ss = [v * ws_ref[...] + bs_ref[...] for v in vs]
    ys = [(_gelu(u) * s * _silu(g)).astype(BF16) for u, s, g in zip(hu, ss, hg)]
    x1s = [x + _dot(y, wout_ref[...]) for x, y in zip(xs, ys)]
    _pe_store(xo_ref, parts, x1s, p_ref, peg_ref, wpg_ref, wpp_ref)


def _p_spec(rows, row, layer):
    return pl.BlockSpec((None, rows, PLE_DIM), lambda *g: (layer,) + row(*g))


def _layer_a(x, p, layer, wts, *, batch, seq):
    n = x.shape[0]
    chunked = seq > 1
    ng, win, lng, lnb, ws, bs, wout, peg, wpg, wpp = wts
    if chunked:
        tiles = seq // TD
        grid = (batch, tiles)
        row = lambda b, t: (b * tiles + t, 0)
        rows = TD
        vo_shape = jax.ShapeDtypeStruct((batch, CHUNK_A, WIDTH_A), F32)
        vo_spec = pl.BlockSpec((1, CHUNK_A, WIDTH_A), lambda b, t: (b, 0, 0))
        scratch = [pltpu.VMEM((TD, WIDTH_A), F32)]
    else:
        grid = (1, 1)
        row = lambda b, t: (0, 0)
        rows = n
        vo_shape = jax.ShapeDtypeStruct((n, WIDTH_A), F32)
        vo_spec = pl.BlockSpec((n, WIDTH_A), row)
        scratch = []
    consts = [ng, win, lng, lnb, ws, bs, wout, peg, wpg, wpp]
    return pl.pallas_call(
        functools.partial(_layer_a_kernel, chunked=chunked),
        out_shape=(jax.ShapeDtypeStruct((n, D_MODEL), F32), vo_shape),
        grid=grid,
        in_specs=[pl.BlockSpec((rows, D_MODEL), row), _p_spec(rows, row, layer)]
        + [_const_spec(c.shape) for c in consts],
        out_specs=(pl.BlockSpec((rows, D_MODEL), row), vo_spec),
        scratch_shapes=scratch,
        compiler_params=_cparams("arbitrary", "arbitrary"),
        name="layer_a_prompt" if chunked else "layer_a_sample",
    )(x, p, *consts)


CONV_ROWS = 64


def _layer_b_prompt_kernel(x_ref, p_ref, ng_ref, wab_ref, wg_ref, cw_ref, cb_ref, lng_ref, lnb_ref, wout_ref,
                           peg_ref, wpg_ref, wpp_ref, xo_ref, st_ref, hext_ref, c_ref):
    t = pl.program_id(1)
    ncb = WIDTH_B // LANES
    npair = wab_ref.shape[0] // 2
    rows = x_ref.shape[0]

    @pl.when(t == 0)
    def _():
        hext_ref[:, 0:CONV_PAD, :] = jnp.zeros((ncb, CONV_PAD, LANES), F32)

    x = x_ref[...]
    xn = _rms(x, ng_ref[...]).astype(BF16)

    def glu_cols(c):
        h2 = _dot(xn, wab_ref[c]) * _sigmoid(_dot(xn, wab_ref[npair + c]))
        hext_ref[2 * c, CONV_PAD:CONV_PAD + rows, :] = h2[:, 0:LANES]
        hext_ref[2 * c + 1, CONV_PAD:CONV_PAD + rows, :] = h2[:, LANES:2 * LANES]

    base = CONV_PAD - (CONV_W - 1)

    def conv_cols(cb):
        for rg in range(rows // CONV_ROWS):
            accs = [None] * (CONV_ROWS // 8)
            for k in range(CONV_W):
                wk = cw_ref[cb, k:k + 1, :]
                for j in range(CONV_ROWS // 8):
                    r0 = rg * CONV_ROWS + j * 8 + base + k
                    term = wk * hext_ref[cb, r0:r0 + 8, :]
                    accs[j] = term if accs[j] is None else accs[j] + term
            for j in range(CONV_ROWS // 8):
                r0 = rg * CONV_ROWS + j * 8
                c_ref[cb, r0:r0 + 8, :] = accs[j]

    glu_cols(0)

    def body(c, carry):
        conv_cols(2 * c)
        conv_cols(2 * c + 1)
        glu_cols(c + 1)
        return carry

    lax.fori_loop(0, npair - 1, body, 0)
    conv_cols(ncb - 2)
    conv_cols(ncb - 1)

    for cb in range(ncb):
        st_ref[0, :, cb * LANES:(cb + 1) * LANES] = hext_ref[cb, rows + base:rows + CONV_PAD, :]
        hext_ref[cb, 0:CONV_PAD, :] = hext_ref[cb, rows:rows + CONV_PAD, :]
    parts = _row_parts(rows, 2)
    sgs = [_silu(_dot(xn[ps, :], wg_ref[...])) for ps in parts]
    cs = [jnp.concatenate([c_ref[cb, ps, :] for cb in range(ncb)], axis=-1) + cb_ref[...] for ps in parts]
    ys = [(_silu(_layer_norm(c, lng_ref[...], lnb_ref[...])) * sg).astype(BF16) for c, sg in zip(cs, sgs)]
    x1s = [x[ps, :] + _dot(y, wout_ref[...]) for ps, y in zip(parts, ys)]
    _pe_store(xo_ref, parts, x1s, p_ref, peg_ref, wpg_ref, wpp_ref)


def _layer_b_prompt(x, p, layer, wts, *, batch, seq):
    n = x.shape[0]
    tiles = seq // TD
    ng, win, cw, cb, lng, lnb, wout, peg, wpg, wpp = wts
    ncb = WIDTH_B // LANES
    cw3 = jnp.transpose(cw.reshape(CONV_W, ncb, LANES), (1, 0, 2))
    wab = jnp.transpose(win[:, 0:2 * WIDTH_B].reshape(D_MODEL, ncb, 2 * LANES), (1, 0, 2))
    consts = [ng, wab, win[:, 2 * WIDTH_B:], cw3, cb, lng, lnb, wout, peg, wpg, wpp]
    row = lambda b, t: (b * tiles + t, 0)
    return pl.pallas_call(
        _layer_b_prompt_kernel,
        out_shape=(jax.ShapeDtypeStruct((n, D_MODEL), F32),
                   jax.ShapeDtypeStruct((batch, CONV_W - 1, WIDTH_B), F32)),
        grid=(batch, tiles),
        in_specs=[pl.BlockSpec((TD, D_MODEL), row), _p_spec(TD, row, layer)]
        + [_const_spec(c.shape) for c in consts],
        out_specs=(pl.BlockSpec((TD, D_MODEL), row),
                   pl.BlockSpec((1, CONV_W - 1, WIDTH_B), lambda b, t: (b, 0, 0))),
        scratch_shapes=[pltpu.VMEM((ncb, TD + CONV_PAD, LANES), F32), pltpu.VMEM((ncb, TD, LANES), F32)],
        compiler_params=_cparams("arbitrary", "arbitrary"),
        name="layer_b_prompt",
    )(x, p, *consts)


SAMPLE_CONV_ROWS = 16


def _layer_b_sample_kernel(x_ref, p_ref, st_ref, ng_ref, win_ref, cw_ref, cb_ref, lng_ref, lnb_ref, wout_ref,
                           peg_ref, wpg_ref, wpp_ref, xo_ref, ho_ref, h_ref, sg_ref, c_ref):
    i = pl.program_id(0)
    w = WIDTH_B

    @pl.when(i == 0)
    def _():
        xn = _rms(x_ref[...], ng_ref[...]).astype(BF16)
        h = _dot(xn, win_ref[:, 0:w]) * _sigmoid(_dot(xn, win_ref[:, w:2 * w]))
        h_ref[...] = h
        ho_ref[...] = h
        sg_ref[...] = _silu(_dot(xn, win_ref[:, 2 * w:3 * w]))

    r0 = pl.multiple_of(i * SAMPLE_CONV_ROWS, SAMPLE_CONV_ROWS)
    acc = cb_ref[...] + cw_ref[CONV_W - 1:CONV_W, :] * h_ref[pl.ds(r0, SAMPLE_CONV_ROWS), :]
    for k in range(CONV_W - 1):
        acc = acc + cw_ref[k:k + 1, :] * st_ref[k]
    c_ref[pl.ds(r0, SAMPLE_CONV_ROWS), :] = acc

    @pl.when(i == pl.num_programs(0) - 1)
    def _():
        c = _silu(_layer_norm(c_ref[...], lng_ref[...], lnb_ref[...]))
        y = (c * sg_ref[...]).astype(BF16)
        x1 = x_ref[...] + _dot(y, wout_ref[...])
        xo_ref[...] = _pe_update(x1, p_ref, peg_ref, wpg_ref, wpp_ref)


def _layer_b_sample(x, p, layer, state, wts):
    n = x.shape[0]
    consts = list(wts)
    full = lambda i: (0, 0)
    return pl.pallas_call(
        _layer_b_sample_kernel,
        out_shape=(jax.ShapeDtypeStruct((n, D_MODEL), F32), jax.ShapeDtypeStruct((n, WIDTH_B), F32)),
        grid=(n // SAMPLE_CONV_ROWS,),
        in_specs=[pl.BlockSpec((n, D_MODEL), full), _p_spec(n, full, layer),
                  pl.BlockSpec((CONV_W - 1, SAMPLE_CONV_ROWS, WIDTH_B), lambda i: (0, i, 0))]
        + [_const_spec(c.shape) for c in consts],
        out_specs=(pl.BlockSpec((n, D_MODEL), full), pl.BlockSpec((n, WIDTH_B), full)),
        scratch_shapes=[pltpu.VMEM((n, WIDTH_B), F32)] * 3,
        compiler_params=_cparams("arbitrary"),
        name="layer_b_sample",
    )(x, p, state, *consts)


AUG = 2 * HEAD_DIM


def _head_rms(xh, g):
    return xh * lax.rsqrt(jnp.mean(xh * xh, axis=-1, keepdims=True) + EPS) * g


def _rope(xh, cos, sin):
    half = ROPE_DIM // 2
    lane = lax.broadcasted_iota(jnp.int32, xh.shape, 1)
    partner = jnp.where(lane < half, pltpu.roll(xh, LANES - half, 1), pltpu.roll(xh, half, 1))
    return xh * cos + partner * sin


def _proj_c_kernel(x_ref, ng_ref, win_ref, qn_ref, kn_ref, cos_ref, sin_ref,
                   q_ref, k_ref, v_ref, sg_ref, *rest, prompt):
    xn = _rms(x_ref[...], ng_ref[...]).astype(BF16)
    rows = xn.shape[0]
    w = WIDTH_ATT
    cos = cos_ref[...]
    sin = sin_ref[...]
    qf = _dot(xn, win_ref[:, 0:w])
    kf = _dot(xn, win_ref[:, w:2 * w])
    v = _dot(xn, win_ref[:, 2 * w:3 * w])
    v_ref[...] = v
    sg_ref[...] = _silu(_dot(xn, win_ref[:, 3 * w:4 * w]))
    if prompt:
        vt_ref, km_ref, qt_ref = rest
        lane = lax.broadcasted_iota(jnp.int32, (rows, HEAD_DIM), 1)
        onehot = jnp.where(lane == pl.program_id(1), 1.0, 0.0).astype(BF16)
    for h in range(N_HEADS):
        hs = slice(h * HEAD_DIM, (h + 1) * HEAD_DIM)
        qh = _rope(_head_rms(qf[:, hs], qn_ref[...]), cos, sin)
        kh = _rope(_head_rms(kf[:, hs], kn_ref[...]), cos, sin)
        k_ref[:, hs] = kh
        if prompt:
            km_ref[0, :, hs] = jnp.mean(kh, axis=0, keepdims=True)
            q_ref[:, h * AUG:h * AUG + HEAD_DIM] = kh.astype(BF16)
            q_ref[:, h * AUG + HEAD_DIM:(h + 1) * AUG] = onehot
            qt_ref[0, 0, h] = jnp.transpose(qh).astype(BF16)
            vt_ref[0, 0, h] = jnp.transpose(v[:, hs]).astype(BF16)
        else:
            q_ref[:, hs] = qh.astype(BF16)


def _proj_c(x, wts, cos, sin, *, batch, seq):
    n = x.shape[0]
    prompt = seq > 1
    rows = TM if prompt else n
    tiles = seq // rows if prompt else 1
    nb = batch if prompt else 1
    consts = list(wts)
    row = lambda b, t: (b * tiles + t, 0)
    pos = lambda b, t: (t, 0)
    wide = jax.ShapeDtypeStruct((n, WIDTH_ATT), F32)
    spec = pl.BlockSpec((rows, WIDTH_ATT), row)
    qw = N_HEADS * AUG if prompt else WIDTH_ATT
    out_shape = [jax.ShapeDtypeStruct((n, qw), BF16), wide, wide, wide]
    out_specs = [pl.BlockSpec((rows, qw), row), spec, spec, spec]
    if prompt:
        per_tile = lambda b, t: (b, t, 0, 0, 0)
        out_shape += [jax.ShapeDtypeStruct((nb, tiles, N_HEADS, HEAD_DIM, rows), BF16),
                      jax.ShapeDtypeStruct((nb * tiles, 1, WIDTH_ATT), F32),
                      jax.ShapeDtypeStruct((nb, tiles, N_HEADS, HEAD_DIM, rows), BF16)]
        out_specs += [pl.BlockSpec((1, 1, N_HEADS, HEAD_DIM, rows), per_tile),
                      pl.BlockSpec((1, 1, WIDTH_ATT), lambda b, t: (b * tiles + t, 0, 0)),
                      pl.BlockSpec((1, 1, N_HEADS, HEAD_DIM, rows), per_tile)]
    return pl.pallas_call(
        functools.partial(_proj_c_kernel, prompt=prompt),
        out_shape=tuple(out_shape),
        grid=(nb, tiles),
        in_specs=[pl.BlockSpec((rows, D_MODEL), row)] + [_const_spec(c.shape) for c in consts]
        + [pl.BlockSpec((rows, HEAD_DIM), pos), pl.BlockSpec((rows, HEAD_DIM), pos)],
        out_specs=tuple(out_specs),
        compiler_params=_cparams("arbitrary", "arbitrary"),
        name="proj_c_prompt" if prompt else "proj_c_sample",
    )(x, *consts, cos, sin)


def _proj_d_kernel(x_ref, ng_ref, win_ref, wf_ref, bf_ref, wft_ref, bft_ref, qn_ref, kn_ref,
                   q_ref, k_ref, v_ref, sg_ref, lf_ref, *rest, prompt):
    xn = _rms(x_ref[...], ng_ref[...]).astype(BF16)
    rows = xn.shape[0]
    w = WIDTH_ATT
    qf = _dot(xn, win_ref[:, 0:w])
    kf = _dot(xn, win_ref[:, w:2 * w])
    v = _dot(xn, win_ref[:, 2 * w:3 * w])
    v_ref[...] = v
    sg_ref[...] = _silu(_dot(xn, win_ref[:, 3 * w:4 * w]))
    lane = lax.broadcasted_iota(jnp.int32, (rows, LANES), 1)
    logf = jnp.where(lane < N_HEADS, _log_sigmoid(_dot(xn, wf_ref[...]) + bf_ref[...]), 0.0)
    lf_ref[...] = logf[:, 0:N_HEADS]
    if not prompt:
        (lft_ref,) = rest
        for h in range(N_HEADS):
            hs = slice(h * HEAD_DIM, (h + 1) * HEAD_DIM)
            q_ref[:, hs] = _head_rms(qf[:, hs], qn_ref[...]).astype(BF16)
            k_ref[:, hs] = _head_rms(kf[:, hs], kn_ref[...])
        lft_ref[...] = _log_sigmoid(_dot_nt(wft_ref[...], xn) + bft_ref[...])
        return

    vt_ref, qt_ref, carry_ref = rest

    @pl.when(pl.program_id(1) == 0)
    def _():
        carry_ref[...] = jnp.zeros((1, LANES), F32)

    r = lax.broadcasted_iota(jnp.int32, (rows, rows), 0)
    c = lax.broadcasted_iota(jnp.int32, (rows, rows), 1)
    tri = jnp.where(c <= r, 1.0, 0.0).astype(BF16)
    hi, mid, lo = _split3(logf)
    cum = _dot(tri, hi) + _dot(tri, mid) + _dot(tri, lo) + carry_ref[...]
    carry_ref[...] = cum[rows - 1:rows, :]
    pieces = [p.astype(F32) for p in _split3(cum * (1.0 / SCALE))]
    pieces_t = [jnp.transpose(p) for p in pieces]
    sub = lax.broadcasted_iota(jnp.int32, (HEAD_DIM, rows), 0)
    for h in range(N_HEADS):
        hs = slice(h * HEAD_DIM, (h + 1) * HEAD_DIM)
        c1, c2, c3 = [p[h:h + 1, :] for p in pieces_t]
        aug_q = jnp.where(sub == 0, c1, jnp.where(sub == 1, c2, jnp.where(sub == 2, c3,
                          jnp.where(sub < 6, 1.0, 0.0))))
        qt_ref[0, 0, h, 0:HEAD_DIM, :] = jnp.transpose(_head_rms(qf[:, hs], qn_ref[...])).astype(BF16)
        qt_ref[0, 0, h, HEAD_DIM:AUG, :] = aug_q.astype(BF16)
        kh = _head_rms(kf[:, hs], kn_ref[...])
        k_ref[:, hs] = kh
        d1, d2, d3 = [p[:, h:h + 1] for p in pieces]
        aug_k = jnp.where(lane == 3, -d1, jnp.where(lane == 4, -d2, jnp.where(lane == 5, -d3,
                          jnp.where(lane < 3, 1.0, 0.0))))
        q_ref[:, h * AUG:h * AUG + HEAD_DIM] = kh.astype(BF16)
        q_ref[:, h * AUG + HEAD_DIM:(h + 1) * AUG] = aug_k.astype(BF16)
        vt_ref[0, 0, h] = jnp.transpose(v[:, hs]).astype(BF16)


def _proj_d(x, wts, *, batch, seq):
    n = x.shape[0]
    prompt = seq > 1
    rows = TM if prompt else n
    tiles = seq // rows if prompt else 1
    nb = batch if prompt else 1
    consts = list(wts)
    row = lambda b, t: (b * tiles + t, 0)
    wide = jax.ShapeDtypeStruct((n, WIDTH_ATT), F32)
    spec = pl.BlockSpec((rows, WIDTH_ATT), row)
    qw = N_HEADS * AUG if prompt else WIDTH_ATT
    out_shape = [jax.ShapeDtypeStruct((n, qw), BF16), wide, wide, wide, jax.ShapeDtypeStruct((n, N_HEADS), F32)]
    out_specs = [pl.BlockSpec((rows, qw), row), spec, spec, spec, pl.BlockSpec((rows, N_HEADS), row)]
    scratch = []
    if prompt:
        per_tile = lambda b, t: (b, t, 0, 0, 0)
        out_shape += [jax.ShapeDtypeStruct((nb, tiles, N_HEADS, HEAD_DIM, rows), BF16),
                      jax.ShapeDtypeStruct((nb, tiles, N_HEADS, AUG, rows), BF16)]
        out_specs += [pl.BlockSpec((1, 1, N_HEADS, HEAD_DIM, rows), per_tile),
                      pl.BlockSpec((1, 1, N_HEADS, AUG, rows), per_tile)]
        scratch = [pltpu.VMEM((1, LANES), F32)]
    else:
        out_shape.append(jax.ShapeDtypeStruct((LANES, n), F32))
        out_specs.append(pl.BlockSpec((LANES, n), lambda b, t: (0, 0)))
    return pl.pallas_call(
        functools.partial(_proj_d_kernel, prompt=prompt),
        out_shape=tuple(out_shape),
        grid=(nb, tiles),
        in_specs=[pl.BlockSpec((rows, D_MODEL), row)] + [_const_spec(c.shape) for c in consts],
        out_specs=tuple(out_specs),
        scratch_shapes=scratch,
        compiler_params=_cparams("arbitrary", "arbitrary"),
        name="proj_d_prompt" if prompt else "proj_d_sample",
    )(x, *consts)


EXP2_SCALE = SCALE * math.log2(math.e)


def _flash_block(j, qa, ka_ref, vt_ref, m_ref, l_ref, acc_ref, causal):
    blk = TM
    r0 = pl.multiple_of(j * blk, blk)
    scores = [_dot(ka_ref[pl.ds(r0, blk), h * AUG:(h + 1) * AUG], qa(h)) for h in range(N_HEADS)]
    probs, alphas = [], []
    for h in range(N_HEADS):
        s = scores[h]
        if causal is not None:
            s = jnp.where(causal, s, NEG)
        m_old = m_ref[h]
        m_new = jnp.maximum(m_old, jnp.max(s, axis=0, keepdims=True))
        alpha = jnp.exp2((m_old - m_new) * EXP2_SCALE)
        p = jnp.exp2((s - m_new) * EXP2_SCALE)
        l_ref[h] = alpha * l_ref[h] + jnp.sum(p, axis=0, keepdims=True)
        m_ref[h] = m_new
        probs.append(p.astype(BF16))
        alphas.append(alpha)
    for h in range(N_HEADS):
        acc_ref[h] = alphas[h] * acc_ref[h] + _dot(vt_ref[0, j, h], probs[h])


def _flash_all(i, qa, ka_ref, vt_ref, sg_ref, y_ref, m_ref, l_ref, acc_ref):
    blk = TM
    m_ref[...] = jnp.full(m_ref.shape, NEG, F32)
    l_ref[...] = jnp.zeros(l_ref.shape, F32)
    acc_ref[...] = jnp.zeros(acc_ref.shape, F32)

    def body(j, carry):
        _flash_block(j, qa, ka_ref, vt_ref, m_ref, l_ref, acc_ref, None)
        return carry

    lax.fori_loop(0, i, body, 0)
    key = lax.broadcasted_iota(jnp.int32, (blk, blk), 0)
    qry = lax.broadcasted_iota(jnp.int32, (blk, blk), 1)
    _flash_block(i, qa, ka_ref, vt_ref, m_ref, l_ref, acc_ref, key <= qry)
    for h in range(N_HEADS):
        hs = slice(h * HEAD_DIM, (h + 1) * HEAD_DIM)
        y_ref[:, hs] = (jnp.transpose(acc_ref[h] / l_ref[h]) * sg_ref[:, hs]).astype(BF16)


def _attn_c_prompt_kernel(qt_ref, ka_ref, vt_ref, km_ref, sg_ref, y_ref, qa_ref, m_ref, l_ref, acc_ref):
    i = pl.program_id(1)
    blk = MOBA_BLOCK
    nblk = km_ref.shape[0]
    sub = lax.broadcasted_iota(jnp.int32, (nblk, blk), 0)
    valid = sub < i
    for h in range(N_HEADS):
        hs = slice(h * HEAD_DIM, (h + 1) * HEAD_DIM)
        qt = qt_ref[0, 0, h]
        km = km_ref[:, hs]
        km_hi = km.astype(BF16)
        km_lo = (km - km_hi.astype(F32)).astype(BF16)
        gate = _dot(km_hi, qt) + _dot(km_lo, qt)
        bias = jnp.zeros((nblk, blk), F32)
        for n in range(nblk):
            gn = gate[n:n + 1, :]
            ahead = valid & ((gate > gn) | ((gate == gn) & (sub < n)))
            rank = jnp.sum(jnp.where(ahead, 1.0, 0.0), axis=0, keepdims=True)
            bias = jnp.where((sub == n) & (rank >= MOBA_TOPK) & valid, NEG, bias)
        bias = jnp.concatenate([bias, jnp.zeros((HEAD_DIM - nblk, blk), F32)], axis=0)
        qa_ref[h, 0:HEAD_DIM, :] = qt
        qa_ref[h, HEAD_DIM:AUG, :] = bias.astype(BF16)
    _flash_all(i, lambda h: qa_ref[h], ka_ref, vt_ref, sg_ref, y_ref, m_ref, l_ref, acc_ref)


def _attn_d_prompt_kernel(qt_ref, ka_ref, vt_ref, sg_ref, y_ref, m_ref, l_ref, acc_ref):
    i = pl.program_id(1)
    _flash_all(i, lambda h: qt_ref[0, 0, h], ka_ref, vt_ref, sg_ref, y_ref, m_ref, l_ref, acc_ref)


def _attn_prompt(qt, ka, vt, sg, km, *, batch, seq):
    n = ka.shape[0]
    tiles = seq // TM
    row = lambda b, t: (b * tiles + t, 0)
    per_b = lambda b, t: (b, 0)
    stat = pltpu.VMEM((N_HEADS, 1, TM), F32)
    in_specs = [pl.BlockSpec((1, 1) + qt.shape[2:], lambda b, t: (b, t, 0, 0, 0)),
                pl.BlockSpec((seq, N_HEADS * AUG), per_b),
                pl.BlockSpec((1, tiles, N_HEADS, HEAD_DIM, TM), lambda b, t: (b, 0, 0, 0, 0))]
    args = [qt, ka, vt]
    scratch = [stat, stat, pltpu.VMEM((N_HEADS, HEAD_DIM, TM), F32)]
    if km is not None:
        in_specs.append(pl.BlockSpec((tiles, WIDTH_ATT), per_b))
        args.append(km)
        scratch = [pltpu.VMEM((N_HEADS, AUG, TM), BF16)] + scratch
    in_specs.append(pl.BlockSpec((TM, WIDTH_ATT), row))
    args.append(sg)
    return pl.pallas_call(
        _attn_d_prompt_kernel if km is None else _attn_c_prompt_kernel,
        out_shape=jax.ShapeDtypeStruct((n, WIDTH_ATT), BF16),
        grid=(batch, tiles),
        in_specs=in_specs,
        out_specs=pl.BlockSpec((TM, WIDTH_ATT), row),
        scratch_shapes=scratch,
        compiler_params=_cparams("arbitrary", "arbitrary"),
        name="attn_d_prompt" if km is None else "attn_c_prompt",
    )(*args)


LOG2E = math.log2(math.e)


def _page_scores(kp, qs, bias=None):
    prod = kp * qs[None]
    if bias is not None:
        prod = prod + bias
    return jnp.sum(prod, axis=-1, keepdims=True)


def _attn_c_sample_kernel(pt_ref, q_ref, kn_ref, vn_ref, sg_ref, *rest, npages):
    k_refs, v_refs = rest[:npages], rest[npages:2 * npages]
    y_ref = rest[2 * npages]
    q = q_ref[0].astype(F32)
    qs = q * (SCALE * LOG2E)
    ppb = MOBA_BLOCK // PAGE_SIZE
    nblk = npages // ppb
    gates, ms, ls, accs = [], [], [], []
    for n in range(nblk):
        kps = [k_refs[ppb * n + r][0, 0] for r in range(ppb)]
        vps = [v_refs[ppb * n + r][0, 0] for r in range(ppb)]
        ss = [_page_scores(kp, qs) for kp in kps]
        ksum = functools.reduce(lambda a, b: a + b, [jnp.sum(kp, axis=0) for kp in kps])
        gates.append(jnp.sum(q * (ksum * (1.0 / MOBA_BLOCK)), axis=-1, keepdims=True))
        m = functools.reduce(jnp.maximum, [jnp.max(s, axis=0) for s in ss])
        es = [jnp.exp2(s - m[None]) for s in ss]
        ms.append(m)
        ls.append(functools.reduce(lambda a, b: a + b, [jnp.sum(e, axis=0) for e in es]))
        accs.append(functools.reduce(lambda a, b: a + b, [jnp.sum(e * vp, axis=0) for e, vp in zip(es, vps)]))
    sels = []
    for n in range(nblk):
        rank = jnp.zeros((N_HEADS, 1), F32)
        for mth in range(nblk):
            if mth == n:
                continue
            ahead = (gates[mth] >= gates[n]) if mth < n else (gates[mth] > gates[n])
            rank = rank + jnp.where(ahead, 1.0, 0.0)
        sels.append(rank < MOBA_TOPK)
    s_self = jnp.sum(qs * kn_ref[0], axis=-1, keepdims=True)
    mx = s_self
    for n in range(nblk):
        mx = jnp.maximum(mx, jnp.where(sels[n], ms[n], NEG))
    wself = jnp.exp2(s_self - mx)
    l = wself
    acc = wself * vn_ref[0]
    for n in range(nblk):
        wgt = jnp.where(sels[n], jnp.exp2(ms[n] - mx), 0.0)
        l = l + wgt * ls[n]
        acc = acc + wgt * accs[n]
    y_ref[0] = ((acc / l) * sg_ref[0]).astype(BF16)


def _attn_d_sample_kernel(pt_ref, q_ref, kn_ref, vn_ref, sg_ref, lft_ref, *rest, npages):
    k_refs, v_refs, lf_refs = rest[:npages], rest[npages:2 * npages], rest[2 * npages:3 * npages]
    y_ref = rest[3 * npages]
    b = pl.program_id(0)
    q = q_ref[0].astype(F32)
    qs = q * (SCALE * LOG2E)
    blocks = [lf_refs[pg][0, 0] for pg in range(npages)]
    if npages * N_HEADS < LANES:
        blocks.append(jnp.zeros((LANES - npages * N_HEADS, PAGE_SIZE), F32))
    lft = jnp.concatenate(blocks, axis=0)
    r = lax.broadcasted_iota(jnp.int32, (LANES, LANES), 0)
    c = lax.broadcasted_iota(jnp.int32, (LANES, LANES), 1)
    after = jnp.where(r > c, 1.0, 0.0).astype(BF16)
    within = sum(_dot(p, after) for p in _split3(lft))
    tot = jnp.broadcast_to(within[:, 0:1] + lft[:, 0:1], (LANES, LANES))
    later = jnp.where((r % N_HEADS == c % N_HEADS) & (c // N_HEADS > r // N_HEADS), 1.0, 0.0).astype(BF16)
    carry = sum(_dot(later, p) for p in _split3(tot))
    lane = lax.broadcasted_iota(jnp.int32, (N_HEADS, lft_ref.shape[1]), 1)
    lf_new = jnp.sum(jnp.where(lane == b, lft_ref[0:N_HEADS, :], 0.0), axis=-1, keepdims=True)
    dec = (within + carry) * LOG2E
    key = lax.broadcasted_iota(jnp.int32, (PAGE_SIZE, N_HEADS, LANES), 0)
    ln3 = lax.broadcasted_iota(jnp.int32, (PAGE_SIZE, N_HEADS, LANES), 2)
    diag = jnp.where(key == ln3, 1.0, 0.0)

    ms, ls, accs = [], [], []
    for pg in range(npages):
        dec_pg = dec[pg * N_HEADS:(pg + 1) * N_HEADS, :] + lf_new * LOG2E
        s = _page_scores(k_refs[pg][0, 0], qs, diag * dec_pg[None])
        m = jnp.max(s, axis=0)
        e = jnp.exp2(s - m[None])
        ms.append(m)
        ls.append(jnp.sum(e, axis=0))
        accs.append(jnp.sum(e * v_refs[pg][0, 0], axis=0))
    s_self = jnp.sum(qs * kn_ref[0], axis=-1, keepdims=True)
    mx = functools.reduce(jnp.maximum, ms, s_self)
    wself = jnp.exp2(s_self - mx)
    l = wself
    acc = wself * vn_ref[0]
    for pg in range(npages):
        wgt = jnp.exp2(ms[pg] - mx)
        l = l + wgt * ls[pg]
        acc = acc + wgt * accs[pg]
    y_ref[0] = ((acc / l) * sg_ref[0]).astype(BF16)


def _attn_sample(q, k_new, v_new, sg, lft, caches, layer, page_table):
    n = q.shape[0]
    npages = page_table.shape[1]
    r3 = lambda a: a.reshape(n, N_HEADS, HEAD_DIM)
    rowspec = pl.BlockSpec((1, N_HEADS, HEAD_DIM), lambda b, pt: (b, 0, 0))
    in_specs = [rowspec] * 4
    args = [r3(q), r3(k_new), r3(v_new), r3(sg)]
    if lft is not None:
        in_specs.append(pl.BlockSpec(lft.shape, lambda b, pt: (0, 0)))
        args.append(lft)
    for cache in caches:
        blk = (1, 1) + cache.shape[2:]
        tail = (0,) * (cache.ndim - 2)
        for pg in range(npages):
            in_specs.append(pl.BlockSpec(blk, lambda b, pt, pg=pg, tail=tail: (layer, pt[b * npages + pg]) + tail))
            args.append(cache)
    body = _attn_c_sample_kernel if lft is None else _attn_d_sample_kernel
    y = pl.pallas_call(
        functools.partial(body, npages=npages),
        out_shape=jax.ShapeDtypeStruct((n, N_HEADS, HEAD_DIM), BF16),
        grid_spec=pltpu.PrefetchScalarGridSpec(
            num_scalar_prefetch=1, grid=(n,), in_specs=in_specs, out_specs=rowspec),
        compiler_params=_cparams("arbitrary"),
        name="attn_c_sample" if lft is None else "attn_d_sample",
    )(page_table.reshape(-1), *args)
    return y.reshape(n, WIDTH_ATT)


def _out_pe_kernel(x_ref, y_ref, p_ref, wout_ref, peg_ref, wpg_ref, wpp_ref, xo_ref):
    rows = x_ref.shape[0]
    parts = _row_parts(rows, 2 if rows % (2 * TM) == 0 else 1)
    x1s = [x_ref[ps, :] + _dot(y_ref[ps, :], wout_ref[...]) for ps in parts]
    _pe_store(xo_ref, parts, x1s, p_ref, peg_ref, wpg_ref, wpp_ref)


def _out_pe(x, y, p, layer, wts, *, rows):
    n = x.shape[0]
    consts = list(wts)
    row = lambda i: (i, 0)
    return pl.pallas_call(
        _out_pe_kernel,
        out_shape=jax.ShapeDtypeStruct((n, D_MODEL), F32),
        grid=(n // rows,),
        in_specs=[pl.BlockSpec((rows, D_MODEL), row), pl.BlockSpec((rows, WIDTH_ATT), row),
                  _p_spec(rows, row, layer)] + [_const_spec(c.shape) for c in consts],
        out_specs=pl.BlockSpec((rows, D_MODEL), row),
        compiler_params=_cparams("arbitrary"),
        name="out_pe",
    )(x, y, p, *consts)


def _rope_tables(pos):
    half = ROPE_DIM // 2
    inv = ROPE_THETA ** (-jnp.arange(half, dtype=F32) * 2.0 / ROPE_DIM)
    ang = pos.astype(F32)[:, None] * inv[None, :]
    cos, sin = jnp.cos(ang), jnp.sin(ang)
    rest = HEAD_DIM - ROPE_DIM
    ones = jnp.ones((pos.shape[0], rest), F32)
    zeros = jnp.zeros((pos.shape[0], rest), F32)
    return jnp.concatenate([cos, cos, ones], axis=-1), jnp.concatenate([-sin, sin, zeros], axis=-1)


def kernel(x_prompt, x_sample, state_conv_b, cache_k_c, cache_v_c, cache_k_d, cache_v_d, cache_logf_d, page_table, p_prompt, p_sample, norm_a, w_in_a, ln_g_a, ln_b_a, w_s_a, b_s_a, w_out_a, norm_b, w_in_b, conv_w_b, conv_b_b, ln_g_b, ln_b_b, w_out_b, norm_c, w_in_c, qn_c, kn_c, w_out_c, norm_d, w_in_d, b_f_d, qn_d, kn_d, w_out_d, pe_norm, w_pe_gate, w_pe_proj):
    bsz, seq, _ = x_prompt.shape
    nsmp = x_sample.shape[0]
    depth = pe_norm.shape[0]
    past_len = page_table.shape[1] * PAGE_SIZE
    row2 = lambda a: a.reshape(1, -1)

    pp = p_prompt.reshape(depth, bsz * seq, PLE_DIM)
    ps = p_sample.reshape(depth, nsmp, PLE_DIM)
    xp = x_prompt.reshape(bsz * seq, D_MODEL)
    xs = x_sample.reshape(nsmp, D_MODEL)
    cos_p, sin_p = _rope_tables(jnp.arange(seq))
    cos_s, sin_s = _rope_tables(jnp.full((nsmp,), past_len))

    outs = {k: [] for k in ("va_p", "va_s", "cb_p", "cb_s", "kc_p", "vc_p", "kc_s", "vc_s",
                            "kd_p", "vd_p", "lf_p", "kd_s", "vd_s", "lf_s")}
    for i in range(depth):
        j, kind = divmod(i, 4)
        pe = (row2(pe_norm[i]), w_pe_gate[i].astype(BF16), w_pe_proj[i].astype(BF16))
        if kind == 0:
            gw = WIDTH_A // GROUPS_A
            base = (row2(norm_a[j]), w_in_a[j].astype(BF16), row2(ln_g_a[j]), row2(ln_b_a[j]))
            tail = (w_out_a[j].astype(BF16),) + pe
            ws_p = jnp.tril(w_s_a[j]).astype(BF16)
            bs_p = jnp.repeat(jnp.transpose(b_s_a[j]), gw, axis=1)
            xp, va = _layer_a(xp, pp, i, base + (ws_p, bs_p) + tail, batch=bsz, seq=seq)
            ws_s = row2(jnp.repeat(w_s_a[j][:, 0, 0], gw))
            bs_s = row2(jnp.repeat(b_s_a[j][:, 0], gw))
            xs, vs = _layer_a(xs, ps, i, base + (ws_s, bs_s) + tail, batch=nsmp, seq=1)
            outs["va_p"].append(va)
            outs["va_s"].append(vs.reshape(nsmp, 1, WIDTH_A))
        elif kind == 1:
            wts = (row2(norm_b[j]), w_in_b[j].astype(BF16), conv_w_b[j], row2(conv_b_b[j]),
                   row2(ln_g_b[j]), row2(ln_b_b[j]), w_out_b[j].astype(BF16)) + pe
            xp, cb = _layer_b_prompt(xp, pp, i, wts, batch=bsz, seq=seq)
            xs, hs = _layer_b_sample(xs, ps, i, jnp.transpose(state_conv_b[j], (1, 0, 2)), wts)
            outs["cb_p"].append(cb)
            outs["cb_s"].append(jnp.concatenate([state_conv_b[j][:, 1:], hs[:, None, :]], axis=1))
        elif kind == 2:
            wts = (row2(norm_c[j]), w_in_c[j].astype(BF16), row2(qn_c[j]), row2(kn_c[j]))
            tail = (w_out_c[j].astype(BF16),) + pe
            ka, k, v, sg, vt, km, qt = _proj_c(xp, wts, cos_p, sin_p, batch=bsz, seq=seq)
            y = _attn_prompt(qt, ka, vt, sg, km.reshape(-1, WIDTH_ATT), batch=bsz, seq=seq)
            xp = _out_pe(xp, y, pp, i, tail, rows=TD)
            outs["kc_p"].append(k.reshape(bsz, seq, N_HEADS, HEAD_DIM))
            outs["vc_p"].append(v.reshape(bsz, seq, N_HEADS, HEAD_DIM))
            q, k, v, sg = _proj_c(xs, wts, cos_s, sin_s, batch=nsmp, seq=1)
            y = _attn_sample(q, k, v, sg, None, (cache_k_c, cache_v_c), j, page_table)
            xs = _out_pe(xs, y, ps, i, tail, rows=nsmp)
            outs["kc_s"].append(k.reshape(nsmp, 1, N_HEADS, HEAD_DIM))
            outs["vc_s"].append(v.reshape(nsmp, 1, N_HEADS, HEAD_DIM))
        else:
            wf = jnp.pad(w_in_d[j][:, 4 * WIDTH_ATT:], ((0, 0), (0, LANES - N_HEADS))).astype(BF16)
            bf = jnp.pad(b_f_d[j], (0, LANES - N_HEADS))
            wts = (row2(norm_d[j]), w_in_d[j][:, :4 * WIDTH_ATT].astype(BF16), wf, row2(bf),
                   jnp.transpose(wf), bf.reshape(LANES, 1), row2(qn_d[j]), row2(kn_d[j]))
            tail = (w_out_d[j].astype(BF16),) + pe
            ka, k, v, sg, lf, vt, qt = _proj_d(xp, wts, batch=bsz, seq=seq)
            y = _attn_prompt(qt, ka, vt, sg, None, batch=bsz, seq=seq)
            xp = _out_pe(xp, y, pp, i, tail, rows=TD)
            outs["kd_p"].append(k.reshape(bsz, seq, N_HEADS, HEAD_DIM))
            outs["vd_p"].append(v.reshape(bsz, seq, N_HEADS, HEAD_DIM))
            outs["lf_p"].append(lf.reshape(bsz, seq, N_HEADS))
            q, k, v, sg, lf, lft = _proj_d(xs, wts, batch=nsmp, seq=1)
            y = _attn_sample(q, k, v, sg, lft, (cache_k_d, cache_v_d, jnp.swapaxes(cache_logf_d, 2, 3)), j, page_table)
            xs = _out_pe(xs, y, ps, i, tail, rows=nsmp)
            outs["kd_s"].append(k.reshape(nsmp, 1, N_HEADS, HEAD_DIM))
            outs["vd_s"].append(v.reshape(nsmp, 1, N_HEADS, HEAD_DIM))
            outs["lf_s"].append(lf.reshape(nsmp, 1, N_HEADS))

    st = lambda key: jnp.stack(outs[key])
    return (xp.reshape(bsz, seq, D_MODEL), xs.reshape(nsmp, 1, D_MODEL),
            st("va_p"), st("va_s"), st("cb_p"), st("cb_s"),
            st("kc_p"), st("vc_p"), st("kc_s"), st("vc_s"),
            st("kd_p"), st("vd_p"), st("lf_p"), st("kd_s"), st("vd_s"), st("lf_s"))
```
